```python
import jax, jax.numpy as jnp
from jax import lax
import numpy as np

D_MODEL = 2048
BATCH = 8
SEQ = 4096
DEPTH = 1

MEM_TOKENS = 256

ML_HEADS = 6
ML_DQK = 128
ML_DV = 256
ML_CONV = 4
ML_CHUNK = 64
RET_HEADS = 6
RET_DQK = 128
RET_DV = 256
RET_CHUNK = 64
XA_HEADS = 4
XA_DH = 256

ROPE_BASE = 10000.0
EPS = 1e-6
N_BRANCH = 3

ML_QK = ML_HEADS * ML_DQK
ML_V = ML_HEADS * ML_DV
RET_QK = RET_HEADS * RET_DQK
RET_V = RET_HEADS * RET_DV
XA_W = XA_HEADS * XA_DH

IN_SIZES = (ML_QK, ML_QK, ML_V, ML_V, ML_V, ML_HEADS, ML_HEADS,
            RET_QK, RET_QK, RET_V, RET_V, XA_W, XA_W, N_BRANCH * D_MODEL)
N_IN = sum(IN_SIZES)

kernel_name = "hybrid_mlstm_retention_memxattn_gated"


def rmsnorm(x, g):
    xf = x.astype(jnp.float32)
    y = xf * lax.rsqrt(jnp.mean(xf * xf, axis=-1, keepdims=True) + EPS)
    return (y * g.astype(jnp.float32)).astype(x.dtype)


def head_layernorm(t, g):
    B, S, H, D = t.shape
    tf = t.astype(jnp.float32)
    mu = jnp.mean(tf, axis=-1, keepdims=True)
    var = jnp.mean(jnp.square(tf - mu), axis=-1, keepdims=True)
    y = ((tf - mu) * lax.rsqrt(var + EPS)).reshape(B, S, H * D)
    return y * g.astype(jnp.float32)


def split_heads(t, n_heads):
    B, S, W = t.shape
    return t.reshape(B, S, n_heads, W // n_heads)


def rope(t, positions):
    half = t.shape[-1] // 2
    freqs = ROPE_BASE ** (-jnp.arange(half, dtype=jnp.float32) / half)
    ang = positions.astype(jnp.float32)[..., None] * freqs
    cos = jnp.cos(ang)[:, :, None, :]
    sin = jnp.sin(ang)[:, :, None, :]
    tf = t.astype(jnp.float32)
    t1, t2 = tf[..., :half], tf[..., half:]
    return jnp.concatenate([t1 * cos - t2 * sin, t1 * sin + t2 * cos], axis=-1)


def causal_dwconv(u, w, b):
    K = w.shape[0]
    S = u.shape[1]
    up = jnp.pad(u, ((0, 0), (K - 1, 0), (0, 0)))
    y = b
    for k in range(K):
        y = y + up[:, k:k + S] * w[k]
    return y


def to_chunks(t, L):
    B, S, H, D = t.shape
    return t.reshape(B, S // L, L, H, D).transpose(1, 0, 3, 2, 4)


def from_chunks(t):
    NC, B, H, L, D = t.shape
    return t.transpose(1, 0, 3, 2, 4).reshape(B, NC * L, H, D)


def mlstm_chunkwise(q, k, v, i_pre, f_pre):
    B, S, H, Dk = q.shape
    Dv = v.shape[-1]
    L = ML_CHUNK
    NC = S // L
    qc = to_chunks(q.astype(jnp.float32) * (Dk ** -0.5), L)
    kc = to_chunks(k.astype(jnp.float32), L)
    vc = to_chunks(v.astype(jnp.float32), L)
    ic = i_pre.astype(jnp.float32).reshape(B, NC, L, H).transpose(1, 0, 3, 2)
    lfc = jax.nn.log_sigmoid(f_pre.astype(jnp.float32)).reshape(B, NC, L, H).transpose(1, 0, 3, 2)
    causal = jnp.tril(jnp.ones((L, L), dtype=bool))

    def step(carry, xs):
        C, n, m = carry
        qb, kb, vb, ib, lfb = xs
        b = jnp.cumsum(lfb, axis=-1)
        g = b[..., -1]
        log_d = jnp.where(causal, b[..., :, None] - b[..., None, :] + ib[..., None, :], -jnp.inf)
        log_inter = b + m[..., None]
        m_row = jnp.maximum(log_inter, jnp.max(log_d, axis=-1))
        d = jnp.exp(log_d - m_row[..., None])
        s = jnp.einsum('bhld,bhsd->bhls', qb, kb) * d
        inter = jnp.exp(log_inter - m_row)
        num = jnp.einsum('bhls,bhsv->bhlv', s, vb) + inter[..., None] * jnp.einsum('bhld,bhdv->bhlv', qb, C)
        den = jnp.sum(s, axis=-1) + inter * jnp.einsum('bhld,bhd->bhl', qb, n)
        den = jnp.maximum(jnp.abs(den), jnp.exp(-m_row))
        h = num / den[..., None]
        log_w = g[..., None] - b + ib
        m_new = jnp.maximum(g + m, jnp.max(log_w, axis=-1))
        w = jnp.exp(log_w - m_new[..., None])
        decay = jnp.exp(g + m - m_new)
        kw = kb * w[..., None]
        C_new = decay[..., None, None] * C + jnp.einsum('bhsd,bhsv->bhdv', kw, vb)
        n_new = decay[..., None] * n + jnp.sum(kw, axis=2)
        return (C_new, n_new, m_new), h

    init = (jnp.zeros((B, H, Dk, Dv), jnp.float32),
            jnp.zeros((B, H, Dk), jnp.float32),
            jnp.zeros((B, H), jnp.float32))
    _, hc = lax.scan(step, init, (qc, kc, vc, ic, lfc))
    return from_chunks(hc)


def retention_chunkwise(q, k, v):
    B, S, H, Dk = q.shape
    Dv = v.shape[-1]
    L = RET_CHUNK
    log_gamma = jnp.asarray(np.log(1.0 - 2.0 ** (-5.0 - np.arange(H))), dtype=jnp.float32)
    pos = jnp.arange(L, dtype=jnp.float32)
    causal = jnp.tril(jnp.ones((L, L), dtype=bool))
    intra = jnp.where(causal, jnp.exp((pos[:, None] - pos[None, :]) * log_gamma[:, None, None]), 0.0)
    q_decay = jnp.exp((pos + 1.0) * log_gamma[:, None])
    k_decay = jnp.exp((L - 1.0 - pos) * log_gamma[:, None])
    chunk_decay = jnp.exp(L * log_gamma)
    qc = to_chunks(q.astype(jnp.float32), L)
    kc = to_chunks(k.astype(jnp.float32) * (Dk ** -0.5), L)
    vc = to_chunks(v.astype(jnp.float32), L)

    def step(R, xs):
        qb, kb, vb = xs
        s = jnp.einsum('bhld,bhsd->bhls', qb, kb) * intra
        o = jnp.einsum('bhls,bhsv->bhlv', s, vb) + q_decay[..., None] * jnp.einsum('bhld,bhdv->bhlv', qb, R)
        R_new = chunk_decay[:, None, None] * R + jnp.einsum('bhsd,bhsv->bhdv', kb * k_decay[..., None], vb)
        return R_new, o

    _, oc = lax.scan(step, jnp.zeros((B, H, Dk, Dv), jnp.float32), (qc, kc, vc))
    return from_chunks(oc)


def memory_cross_attention(q, mk, mv):
    B, S, H, D = q.shape
    scores = jnp.einsum('bshd,bmhd->bhsm', q.astype(jnp.float32), mk.astype(jnp.float32)) * (D ** -0.5)
    p = jax.nn.softmax(scores, axis=-1)
    o = jnp.einsum('bhsm,bmhd->bshd', p, mv.astype(jnp.float32))
    return o.reshape(B, S, H * D)


def setup_inputs(seed: int = 0) -> dict:
    key = jax.random.key(seed)
    ks = jax.random.split(key, 17)
    f32 = jnp.float32

    def nrm(k, shape, scale):
        return jax.random.normal(k, shape, f32) * scale

    x = nrm(ks[0], (BATCH, SEQ, D_MODEL), 1.0)
    mem = nrm(ks[1], (BATCH, MEM_TOKENS, D_MODEL), 1.0)
    positions = jnp.broadcast_to(jnp.arange(SEQ, dtype=jnp.int32), (BATCH, SEQ))
    ln_g = 1.0 + nrm(ks[2], (DEPTH, D_MODEL), 0.02)
    mem_ln_g = 1.0 + nrm(ks[3], (DEPTH, D_MODEL), 0.02)
    w_in = nrm(ks[4], (DEPTH, D_MODEL, N_IN), D_MODEL ** -0.5)
    f_off = sum(IN_SIZES[:6])
    b_in = nrm(ks[5], (DEPTH, N_IN), 0.01)
    b_in = b_in.at[:, f_off:f_off + ML_HEADS].add(jnp.linspace(3.0, 6.0, ML_HEADS, dtype=f32))
    conv_w = nrm(ks[6], (DEPTH, ML_CONV, 2 * ML_QK), ML_CONV ** -0.5)
    conv_b = nrm(ks[7], (DEPTH, 2 * ML_QK), 0.01)
    ml_hnorm_g = 1.0 + nrm(ks[8], (DEPTH, ML_V), 0.02)
    ret_hnorm_g = 1.0 + nrm(ks[9], (DEPTH, RET_V), 0.02)
    w_mem_kv = nrm(ks[10], (DEPTH, D_MODEL, 2 * XA_W), D_MODEL ** -0.5)
    w_br_ml = nrm(ks[11], (DEPTH, ML_V, D_MODEL), ML_V ** -0.5)
    w_br_ret = nrm(ks[12], (DEPTH, RET_V, D_MODEL), RET_V ** -0.5)
    w_br_xa = nrm(ks[13], (DEPTH, XA_W, D_MODEL), XA_W ** -0.5)
    w_out = nrm(ks[14], (DEPTH, D_MODEL, D_MODEL), D_MODEL ** -0.5)
    final_g = 1.0 + nrm(ks[15], (D_MODEL,), 0.02)
    return {"x": x, "mem": mem, "positions": positions, "ln_g": ln_g, "mem_ln_g": mem_ln_g,
            "w_in": w_in, "b_in": b_in, "conv_w": conv_w, "conv_b": conv_b,
            "ml_hnorm_g": ml_hnorm_g, "ret_hnorm_g": ret_hnorm_g, "w_mem_kv": w_mem_kv,
            "w_br_ml": w_br_ml, "w_br_ret": w_br_ret, "w_br_xa": w_br_xa, "w_out": w_out,
            "final_g": final_g}


def reference(x, mem, positions, ln_g, mem_ln_g, w_in, b_in, conv_w, conv_b, ml_hnorm_g,
              ret_hnorm_g, w_mem_kv, w_br_ml, w_br_ret, w_br_xa, w_out, final_g):
    B, S, _ = x.shape
    split_at = np.cumsum(IN_SIZES)[:-1].tolist()
    for layer in range(DEPTH):
        h = rmsnorm(x, ln_g[layer])
        proj = h @ w_in[layer] + b_in[layer]
        (ml_q, ml_k, ml_v, ml_o, ml_z, ml_i, ml_f,
         rt_q, rt_k, rt_v, rt_z, xa_q, xa_z, gate_pre) = jnp.split(proj, split_at, axis=-1)

        qk = jax.nn.silu(causal_dwconv(jnp.concatenate([ml_q, ml_k], axis=-1), conv_w[layer], conv_b[layer]))
        ml_qc, ml_kc = jnp.split(qk, 2, axis=-1)
        ml_h = mlstm_chunkwise(split_heads(ml_qc, ML_HEADS), split_heads(ml_kc, ML_HEADS),
                               split_heads(ml_v, ML_HEADS), ml_i, ml_f)
        ml_out = (head_layernorm(ml_h, ml_hnorm_g[layer]) * jax.nn.sigmoid(ml_o.astype(jnp.float32))
                  * jax.nn.silu(ml_z.astype(jnp.float32))).astype(x.dtype)

        rq = rope(split_heads(rt_q, RET_HEADS), positions)
        rk = rope(split_heads(rt_k, RET_HEADS), positions)
        rt_h = retention_chunkwise(rq, rk, split_heads(rt_v, RET_HEADS))
        rt_out = (head_layernorm(rt_h, ret_hnorm_g[layer])
                  * jax.nn.silu(rt_z.astype(jnp.float32))).astype(x.dtype)

        mn = rmsnorm(mem, mem_ln_g[layer])
        mk, mv = jnp.split(mn @ w_mem_kv[layer], 2, axis=-1)
        xa_h = memory_cross_attention(split_heads(xa_q, XA_HEADS), split_heads(mk, XA_HEADS),
                                      split_heads(mv, XA_HEADS))
        xa_out = (xa_h * jax.nn.silu(xa_z.astype(jnp.float32))).astype(x.dtype)

        gates = jax.nn.sigmoid(gate_pre).reshape(B, S, N_BRANCH, D_MODEL)
        merged = (gates[:, :, 0] * (ml_out @ w_br_ml[layer])
                  + gates[:, :, 1] * (rt_out @ w_br_ret[layer])
                  + gates[:, :, 2] * (xa_out @ w_br_xa[layer]))
        x = x + merged @ w_out[layer]
    return rmsnorm(x, final_g)
```

```python
import functools

import numpy as np
import jax
import jax.numpy as jnp
from jax import lax
from jax.experimental import pallas as pl
from jax.experimental.pallas import tpu as pltpu

F32 = jnp.float32
BF16 = jnp.bfloat16

D_MODEL = 2048
MEM_TOKENS = 256
ML_HEADS = 6
ML_DQK = 128
ML_DV = 256
ML_CONV = 4
RET_HEADS = 6
RET_DQK = 128
RET_DV = 256
XA_HEADS = 4
XA_DH = 256
ROPE_BASE = 10000.0
EPS = 1e-6
N_BRANCH = 3

ML_QK = ML_HEADS * ML_DQK
ML_V = ML_HEADS * ML_DV
RET_QK = RET_HEADS * RET_DQK
RET_V = RET_HEADS * RET_DV
XA_W = XA_HEADS * XA_DH

OFF_ML_Q = 0
OFF_ML_K = OFF_ML_Q + ML_QK
OFF_ML_V = OFF_ML_K + ML_QK
OFF_ML_O = OFF_ML_V + ML_V
OFF_ML_Z = OFF_ML_O + ML_V
OFF_IF = OFF_ML_Z + ML_V
OFF_RT_Q = OFF_ML_Z + ML_V
OFF_RT_K = OFF_RT_Q + RET_QK
OFF_RT_V = OFF_RT_K + RET_QK
OFF_RT_Z = OFF_RT_V + RET_V
OFF_XA_Q = OFF_RT_Z + RET_V
OFF_XA_Z = OFF_XA_Q + XA_W
OFF_GATE = OFF_XA_Z + XA_W
N_MAIN = OFF_GATE + N_BRANCH * D_MODEL

CHUNK = 256
GATE_ROWS = 16
F_ROW = 8
CONV_PAD = 8

VMEM_LIMIT = 56 * 1024 * 1024


def _params(sem):
    return pltpu.CompilerParams(dimension_semantics=sem, vmem_limit_bytes=VMEM_LIMIT)


def _sigmoid(x):
    return 0.5 * jnp.tanh(0.5 * x) + 0.5


def _rms(xf, g):
    return xf * lax.rsqrt(jnp.mean(xf * xf, axis=-1, keepdims=True) + EPS) * g


def _head_norm(t, g):
    mu = jnp.mean(t, axis=-1, keepdims=True)
    c = t - mu
    var = jnp.mean(c * c, axis=-1, keepdims=True)
    return c * lax.rsqrt(var + EPS) * g


def _dot(a, b):
    return jnp.dot(a, b, preferred_element_type=F32)


def _memkv_kernel(m_ref, g_ref, w_ref, o_ref):
    h = _rms(m_ref[...], g_ref[...])
    o_ref[...] = _dot(h.astype(BF16), w_ref[...]).astype(BF16)


def _memkv(mem2, g, w):
    rows = mem2.shape[0]
    tm = min(512, rows)
    return pl.pallas_call(
        _memkv_kernel,
        grid=(rows // tm,),
        in_specs=[pl.BlockSpec((tm, D_MODEL), lambda i: (i, 0)),
                  pl.BlockSpec((1, D_MODEL), lambda i: (0, 0)),
                  pl.BlockSpec((D_MODEL, 2 * XA_W), lambda i: (0, 0))],
        out_specs=pl.BlockSpec((tm, 2 * XA_W), lambda i: (i, 0)),
        out_shape=jax.ShapeDtypeStruct((rows, 2 * XA_W), BF16),
        compiler_params=_params(("arbitrary",)),
        name="memkv",
    )(mem2, g, w)


def _inproj_kernel(x_ref, g_ref, w_ref, b_ref, wif_ref, bif_ref, p_ref, gate_ref, hb_ref):
    @pl.when(pl.program_id(1) == 0)
    def _():
        hb = _rms(x_ref[...], g_ref[...]).astype(BF16)
        hb_ref[...] = hb
        gt = lax.dot_general(wif_ref[...], hb, (((1,), (1,)), ((), ())), preferred_element_type=F32)
        gate_ref[...] = gt + bif_ref[...]

    p_ref[...] = (_dot(hb_ref[...], w_ref[...]) + b_ref[...]).astype(BF16)


def _inproj(x2, g, w_main, b_main, wif_t, bif_t, tm, tn):
    t = x2.shape[0]
    return pl.pallas_call(
        _inproj_kernel,
        grid=(t // tm, N_MAIN // tn),
        in_specs=[pl.BlockSpec((tm, D_MODEL), lambda i, j: (i, 0)),
                  pl.BlockSpec((1, D_MODEL), lambda i, j: (0, 0)),
                  pl.BlockSpec((D_MODEL, tn), lambda i, j: (0, j)),
                  pl.BlockSpec((1, tn), lambda i, j: (0, j)),
                  pl.BlockSpec((GATE_ROWS, D_MODEL), lambda i, j: (0, 0)),
                  pl.BlockSpec((GATE_ROWS, 1), lambda i, j: (0, 0))],
        out_specs=[pl.BlockSpec((tm, tn), lambda i, j: (i, j)),
                   pl.BlockSpec((GATE_ROWS, tm), lambda i, j: (0, i))],
        out_shape=[jax.ShapeDtypeStruct((t, N_MAIN), BF16),
                   jax.ShapeDtypeStruct((GATE_ROWS, t), F32)],
        scratch_shapes=[pltpu.VMEM((tm, D_MODEL), BF16)],
        compiler_params=_params(("arbitrary", "arbitrary")),
        name="inproj",
    )(x2, g, w_main, b_main, wif_t, bif_t)


def _rope_kernel(pos_ref, freq_ref, sign_ref, cos_ref, sin_ref):
    ang = pos_ref[...].astype(F32) * freq_ref[...]
    cos_ref[...] = jnp.cos(ang)
    sin_ref[...] = jnp.sin(ang) * sign_ref[...]


def _rope_tables(pos_col, freq2, sign2):
    t = pos_col.shape[0]
    tm = min(2048, t)
    return pl.pallas_call(
        _rope_kernel,
        grid=(t // tm,),
        in_specs=[pl.BlockSpec((tm, 1), lambda i: (i, 0)),
                  pl.BlockSpec((1, RET_DQK), lambda i: (0, 0)),
                  pl.BlockSpec((1, RET_DQK), lambda i: (0, 0))],
        out_specs=[pl.BlockSpec((tm, RET_DQK), lambda i: (i, 0)),
                   pl.BlockSpec((tm, RET_DQK), lambda i: (i, 0))],
        out_shape=[jax.ShapeDtypeStruct((t, RET_DQK), F32),
                   jax.ShapeDtypeStruct((t, RET_DQK), F32)],
        compiler_params=_params(("arbitrary",)),
        name="rope",
    )(pos_col, freq2, sign2)


def _split3(x):
    hi = x.astype(BF16)
    r1 = x - hi.astype(F32)
    mid = r1.astype(BF16)
    lo = (r1 - mid.astype(F32)).astype(BF16)
    return hi, mid, lo


def _mlstm_kernel(q_ref, k_ref, v_ref, o_ref, z_ref, gi_ref, gf_ref, cwq_ref, cwk_ref, cbq_ref, cbk_ref,
                  hg_ref, out_ref, pad_ref, qc_ref, kc_ref, b_ref, c_ref, n_ref, m_ref, *, seq):
    L = CHUNK
    nchunk = seq // L

    pad_ref[pl.ds(0, CONV_PAD), :] = jnp.zeros((CONV_PAD, 2 * ML_DQK), F32)
    pad_ref[pl.ds(CONV_PAD, seq), pl.ds(0, ML_DQK)] = q_ref[...].astype(F32)
    pad_ref[pl.ds(CONV_PAD, seq), pl.ds(ML_DQK, ML_DQK)] = k_ref[...].astype(F32)
    cw = jnp.concatenate([cwq_ref[...], cwk_ref[...]], axis=-1)
    cb = jnp.concatenate([cbq_ref[...], cbk_ref[...]], axis=-1)
    blk = min(512, seq)
    for r in range(0, seq, blk):
        y = cb
        for j in range(ML_CONV):
            y = y + pad_ref[pl.ds(r + CONV_PAD - (ML_CONV - 1) + j, blk), :] * cw[j:j + 1, :]
        y = y * _sigmoid(y)
        qc_ref[pl.ds(r, blk), :] = y[:, :ML_DQK] * (ML_DQK ** -0.5)
        kc_ref[pl.ds(r, blk), :] = y[:, ML_DQK:]

    f = gf_ref[0]
    lf = jnp.minimum(f, 0.0) - jnp.log1p(jnp.exp(-jnp.abs(f)))
    row = lax.broadcasted_iota(jnp.int32, (L, L), 0)
    col = lax.broadcasted_iota(jnp.int32, (L, L), 1)
    triu = (row <= col).astype(BF16)
    hi, mid, lo = _split3(lf)
    b_ref[...] = _dot(hi, triu) + _dot(mid, triu) + _dot(lo, triu)
    causal = row >= col

    c_ref[...] = jnp.zeros_like(c_ref)
    n_ref[...] = jnp.zeros_like(n_ref)
    m_ref[...] = jnp.zeros_like(m_ref)

    def chunk(c, carry):
        r0 = pl.multiple_of(c * L, L)
        rows = pl.ds(r0, L)
        b_row = b_ref[pl.ds(c, 1), :]
        i_row = gi_ref[0, pl.ds(c, 1), :]
        g = b_row[:, L - 1:L]
        br = jnp.broadcast_to(b_row, (L, L))
        bc = br.T
        b_col = bc[:, 0:1]
        log_d = jnp.where(causal, bc - br + i_row, -jnp.inf)
        m_prev = m_ref[...]
        log_inter = b_col + m_prev
        m_row = jnp.maximum(log_inter, jnp.max(log_d, axis=-1, keepdims=True))
        d = jnp.exp(log_d - m_row)
        qf = qc_ref[rows, :]
        kf = kc_ref[rows, :]
        qb = qf.astype(BF16)
        kt = kf.T
        vb = v_ref[rows, :]
        s = _dot(qb, kt.astype(BF16)) * d
        inter = jnp.exp(log_inter - m_row)
        num = _dot(s.astype(BF16), vb) + inter * _dot(qb, c_ref[...].astype(BF16))
        den = jnp.sum(s, axis=-1, keepdims=True) + inter * jnp.sum(qf * n_ref[...], axis=-1, keepdims=True)
        den = jnp.maximum(jnp.abs(den), jnp.exp(-m_row))
        hh = num * (1.0 / den)

        log_w = g - b_row + i_row
        m_new = jnp.maximum(g + m_prev, jnp.max(log_w, axis=-1, keepdims=True))
        w_row = jnp.exp(log_w - m_new)
        decay = jnp.exp(g + m_prev - m_new)
        c_ref[...] = decay * c_ref[...] + _dot((kt * w_row).astype(BF16), vb)
        w8 = jnp.broadcast_to(w_row, (8, L)).astype(BF16)
        n_ref[...] = decay * n_ref[...] + _dot(w8, kf.astype(BF16))[0:1, :]
        m_ref[...] = m_new

        y = _head_norm(hh, hg_ref[...])
        zf = z_ref[rows, :].astype(F32)
        y = y * _sigmoid(o_ref[rows, :].astype(F32)) * (zf * _sigmoid(zf))
        out_ref[rows, :] = y.astype(BF16)
        return carry

    lax.fori_loop(0, nchunk, chunk, 0)


def _mlstm(p, gates3, conv_w, conv_b, hnorm_g, batch, seq):
    t = batch * seq
    nchunk = seq // CHUNK
    qb, vb = ML_DQK, ML_DV

    def pspec(width, off):
        return pl.BlockSpec((seq, width), lambda b, h, off=off // width: (b, off + h))

    def gspec(row0):
        return pl.BlockSpec((1, nchunk, CHUNK), lambda b, h, row0=row0: (row0 + h, b, 0))

    kern = functools.partial(_mlstm_kernel, seq=seq)
    return pl.pallas_call(
        kern,
        grid=(batch, ML_HEADS),
        in_specs=[pspec(qb, OFF_ML_Q), pspec(qb, OFF_ML_K), pspec(vb, OFF_ML_V), pspec(vb, OFF_ML_O),
                  pspec(vb, OFF_ML_Z), gspec(0), gspec(F_ROW),
                  pl.BlockSpec((ML_CONV, qb), lambda b, h: (0, h)),
                  pl.BlockSpec((ML_CONV, qb), lambda b, h: (0, ML_HEADS + h)),
                  pl.BlockSpec((1, qb), lambda b, h: (0, h)),
                  pl.BlockSpec((1, qb), lambda b, h: (0, ML_HEADS + h)),
                  pl.BlockSpec((1, vb), lambda b, h: (0, h))],
        out_specs=pl.BlockSpec((seq, vb), lambda b, h: (b, h)),
        out_shape=jax.ShapeDtypeStruct((t, ML_V), BF16),
        scratch_shapes=[pltpu.VMEM((seq + CONV_PAD, 2 * qb), F32),
                        pltpu.VMEM((seq, qb), F32),
                        pltpu.VMEM((seq, qb), F32),
                        pltpu.VMEM((nchunk, CHUNK), F32),
                        pltpu.VMEM((qb, vb), F32),
                        pltpu.VMEM((1, qb), F32),
                        pltpu.VMEM((1, 1), F32)],
        compiler_params=_params(("arbitrary", "arbitrary")),
        name="mlstm",
    )(p, p, p, p, p, gates3, gates3, conv_w, conv_w, conv_b, conv_b, hnorm_g)


def _ret_kernel(lg_ref, q_ref, k_ref, v_ref, z_ref, cos_ref, sin_ref, hg_ref, out_ref, r_ref, *, seq):
    L = CHUNK
    nchunk = seq // L
    lg = lg_ref[pl.program_id(1)]
    row = lax.broadcasted_iota(jnp.int32, (L, L), 0)
    col = lax.broadcasted_iota(jnp.int32, (L, L), 1)
    intra = jnp.where(row >= col, jnp.exp((row - col).astype(F32) * lg), 0.0)
    pos_col = lax.broadcasted_iota(jnp.int32, (L, 1), 0).astype(F32)
    pos_row = lax.broadcasted_iota(jnp.int32, (1, L), 1).astype(F32)
    q_decay = jnp.exp((pos_col + 1.0) * lg)
    k_decay = jnp.exp((L - 1.0 - pos_row) * lg)
    chunk_decay = jnp.exp(jnp.full((1, 1), float(L), F32) * lg)
    r_ref[...] = jnp.zeros_like(r_ref)
    half = RET_DQK // 2

    def chunk(c, carry):
        r0 = pl.multiple_of(c * L, L)
        rows = pl.ds(r0, L)
        cs = cos_ref[rows, :]
        sn = sin_ref[rows, :]
        qf = q_ref[rows, :].astype(F32)
        kf = k_ref[rows, :].astype(F32)
        qr = qf * cs + pltpu.roll(qf, half, 1) * sn
        kr = (kf * cs + pltpu.roll(kf, half, 1) * sn) * (RET_DQK ** -0.5)
        qb = qr.astype(BF16)
        kt = kr.T
        vb = v_ref[rows, :]
        s = _dot(qb, kt.astype(BF16)) * intra
        o = _dot(s.astype(BF16), vb) + q_decay * _dot(qb, r_ref[...].astype(BF16))
        r_ref[...] = chunk_decay * r_ref[...] + _dot((kt * k_decay).astype(BF16), vb)
        y = _head_norm(o, hg_ref[...])
        zf = z_ref[rows, :].astype(F32)
        out_ref[rows, :] = (y * (zf * _sigmoid(zf))).astype(BF16)
        return carry

    lax.fori_loop(0, nchunk, chunk, 0)


def _retention(log_gamma, p, cos2, sin2, hnorm_g, batch, seq):
    t = batch * seq
    qb, vb = RET_DQK, RET_DV

    def pspec(width, off):
        return pl.BlockSpec((seq, width), lambda b, h, lg, off=off // width: (b, off + h))

    kern = functools.partial(_ret_kernel, seq=seq)
    grid_spec = pltpu.PrefetchScalarGridSpec(
        num_scalar_prefetch=1,
        grid=(batch, RET_HEADS),
        in_specs=[pspec(qb, OFF_RT_Q), pspec(qb, OFF_RT_K), pspec(vb, OFF_RT_V), pspec(vb, OFF_RT_Z),
                  pl.BlockSpec((seq, qb), lambda b, h, lg: (b, 0)),
                  pl.BlockSpec((seq, qb), lambda b, h, lg: (b, 0)),
                  pl.BlockSpec((1, vb), lambda b, h, lg: (0, h))],
        out_specs=pl.BlockSpec((seq, vb), lambda b, h, lg: (b, h)),
        scratch_shapes=[pltpu.VMEM((qb, vb), F32)],
    )
    return pl.pallas_call(
        kern,
        grid_spec=grid_spec,
        out_shape=jax.ShapeDtypeStruct((t, RET_V), BF16),
        compiler_params=_params(("arbitrary", "arbitrary")),
        name="retention",
    )(log_gamma, p, p, p, p, cos2, sin2, hnorm_g)


def _xattn_kernel(q_ref, z_ref, mk_ref, mv_ref, out_ref):
    sc = lax.dot_general(q_ref[...], mk_ref[...], (((1,), (1,)), ((), ())), preferred_element_type=F32)
    sc = sc * (XA_DH ** -0.5)
    e = jnp.exp(sc - jnp.max(sc, axis=-1, keepdims=True))
    o = _dot(e.astype(BF16), mv_ref[...]) * (1.0 / jnp.sum(e, axis=-1, keepdims=True))
    zf = z_ref[...].astype(F32)
    out_ref[...] = (o * (zf * _sigmoid(zf))).astype(BF16)


def _xattn(p, memkv, batch, seq):
    t = batch * seq
    ts = min(1024, seq)
    ns = seq // ts
    w = XA_DH
    return pl.pallas_call(
        _xattn_kernel,
        grid=(batch, XA_HEADS, ns),
        in_specs=[pl.BlockSpec((ts, w), lambda b, h, s: (b * ns + s, OFF_XA_Q // w + h)),
                  pl.BlockSpec((ts, w), lambda b, h, s: (b * ns + s, OFF_XA_Z // w + h)),
                  pl.BlockSpec((MEM_TOKENS, w), lambda b, h, s: (b, h)),
                  pl.BlockSpec((MEM_TOKENS, w), lambda b, h, s: (b, XA_HEADS + h))],
        out_specs=pl.BlockSpec((ts, w), lambda b, h, s: (b * ns + s, h)),
        out_shape=jax.ShapeDtypeStruct((t, XA_W), BF16),
        compiler_params=_params(("arbitrary", "arbitrary", "arbitrary")),
        name="xattn",
    )(p, p, memkv, memkv)


def _merge_kernel(ml_ref, rt_ref, xa_ref, wml_ref, wrt_ref, wxa_ref, g0_ref, g1_ref, g2_ref, o_ref):
    acc = _sigmoid(g0_ref[...].astype(F32)) * _dot(ml_ref[...], wml_ref[...])
    acc = acc + _sigmoid(g1_ref[...].astype(F32)) * _dot(rt_ref[...], wrt_ref[...])
    acc = acc + _sigmoid(g2_ref[...].astype(F32)) * _dot(xa_ref[...], wxa_ref[...])
    o_ref[...] = acc.astype(BF16)


def _merge(ml_out, rt_out, xa_out, w_ml, w_rt, w_xa, p, tm, tn):
    t = ml_out.shape[0]
    nj = D_MODEL // tn

    def gspec(br):
        return pl.BlockSpec((tm, tn), lambda i, j, o=(OFF_GATE + br * D_MODEL) // tn: (i, o + j))

    return pl.pallas_call(
        _merge_kernel,
        grid=(t // tm, nj),
        in_specs=[pl.BlockSpec((tm, ML_V), lambda i, j: (i, 0)),
                  pl.BlockSpec((tm, RET_V), lambda i, j: (i, 0)),
                  pl.BlockSpec((tm, XA_W), lambda i, j: (i, 0)),
                  pl.BlockSpec((ML_V, tn), lambda i, j: (0, j)),
                  pl.BlockSpec((RET_V, tn), lambda i, j: (0, j)),
                  pl.BlockSpec((XA_W, tn), lambda i, j: (0, j)),
                  gspec(0), gspec(1), gspec(2)],
        out_specs=pl.BlockSpec((tm, tn), lambda i, j: (i, j)),
        out_shape=jax.ShapeDtypeStruct((t, D_MODEL), BF16),
        compiler_params=_params(("arbitrary", "arbitrary")),
        name="merge",
    )(ml_out, rt_out, xa_out, w_ml, w_rt, w_xa, p, p, p)


def _outproj_kernel(x_ref, m_ref, w_ref, g_ref, o_ref):
    y = x_ref[...] + _dot(m_ref[...], w_ref[...])
    o_ref[...] = _rms(y, g_ref[...])


def _outproj(x2, merged, w_out, final_g, tm):
    t = x2.shape[0]
    return pl.pallas_call(
        _outproj_kernel,
        grid=(t // tm,),
        in_specs=[pl.BlockSpec((tm, D_MODEL), lambda i: (i, 0)),
                  pl.BlockSpec((tm, D_MODEL), lambda i: (i, 0)),
                  pl.BlockSpec((D_MODEL, D_MODEL), lambda i: (0, 0)),
                  pl.BlockSpec((1, D_MODEL), lambda i: (0, 0))],
        out_specs=pl.BlockSpec((tm, D_MODEL), lambda i: (i, 0)),
        out_shape=jax.ShapeDtypeStruct((t, D_MODEL), F32),
        compiler_params=_params(("arbitrary",)),
        name="outproj",
    )(x2, merged, w_out, final_g)


def kernel(x, mem, positions, ln_g, mem_ln_g, w_in, b_in, conv_w, conv_b, ml_hnorm_g, ret_hnorm_g, w_mem_kv,
           w_br_ml, w_br_ret, w_br_xa, w_out, final_g):
    batch, seq, _ = x.shape
    assert seq % CHUNK == 0 and ln_g.shape[0] == 1
    t = batch * seq
    x2 = x.reshape(t, D_MODEL)

    w0, b0 = w_in[0], b_in[0]
    w_main = jnp.concatenate([w0[:, :OFF_IF], w0[:, OFF_IF + 2 * ML_HEADS:]], axis=1).astype(BF16)
    b_main = jnp.concatenate([b0[:OFF_IF], b0[OFF_IF + 2 * ML_HEADS:]])[None, :]
    wif_t = jnp.zeros((GATE_ROWS, D_MODEL), F32)
    wif_t = wif_t.at[0:ML_HEADS].set(w0[:, OFF_IF:OFF_IF + ML_HEADS].T)
    wif_t = wif_t.at[F_ROW:F_ROW + ML_HEADS].set(w0[:, OFF_IF + ML_HEADS:OFF_IF + 2 * ML_HEADS].T).astype(BF16)
    bif_t = jnp.zeros((GATE_ROWS, 1), F32)
    bif_t = bif_t.at[0:ML_HEADS, 0].set(b0[OFF_IF:OFF_IF + ML_HEADS])
    bif_t = bif_t.at[F_ROW:F_ROW + ML_HEADS, 0].set(b0[OFF_IF + ML_HEADS:OFF_IF + 2 * ML_HEADS])

    memkv = _memkv(mem.reshape(batch * MEM_TOKENS, D_MODEL), mem_ln_g[0][None, :], w_mem_kv[0].astype(BF16))

    tm = min(1024, t)
    p, gates_t = _inproj(x2, ln_g[0][None, :], w_main, b_main, wif_t, bif_t, tm, 512)
    gates3 = gates_t.reshape(GATE_ROWS, t // CHUNK, CHUNK)

    half = RET_DQK // 2
    freqs = ROPE_BASE ** (-jnp.arange(half, dtype=F32) / half)
    freq2 = jnp.concatenate([freqs, freqs])[None, :]
    sign2 = jnp.concatenate([-jnp.ones((half,), F32), jnp.ones((half,), F32)])[None, :]
    cos2, sin2 = _rope_tables(positions.reshape(t, 1), freq2, sign2)
    log_gamma = jnp.asarray(np.log(1.0 - 2.0 ** (-5.0 - np.arange(RET_HEADS))), dtype=F32)

    ml_out = _mlstm(p, gates3, conv_w[0], conv_b[0][None, :], ml_hnorm_g[0][None, :], batch, seq)
    rt_out = _retention(log_gamma, p, cos2, sin2, ret_hnorm_g[0][None, :], batch, seq)
    xa_out = _xattn(p, memkv, batch, seq)

    merged = _merge(ml_out, rt_out, xa_out, w_br_ml[0].astype(BF16), w_br_ret[0].astype(BF16),
                    w_br_xa[0].astype(BF16), p, tm, 512)
    out = _outproj(x2, merged, w_out[0].astype(BF16), final_g[None, :], min(512, t))
    return out.reshape(batch, seq, D_MODEL)
```

```python
import functools

import numpy as np
import jax
import jax.numpy as jnp
from jax import lax
from jax.experimental import pallas as pl
from jax.experimental.pallas import tpu as pltpu

F32 = jnp.float32
BF16 = jnp.bfloat16

D_MODEL = 2048
MEM_TOKENS = 256
ML_HEADS = 6
ML_DQK = 128
ML_DV = 256
ML_CONV = 4
RET_HEADS = 6
RET_DQK = 128
RET_DV = 256
XA_HEADS = 4
XA_DH = 256
ROPE_BASE = 10000.0
EPS = 1e-6
N_BRANCH = 3

ML_QK = ML_HEADS * ML_DQK
ML_V = ML_HEADS * ML_DV
RET_QK = RET_HEADS * RET_DQK
RET_V = RET_HEADS * RET_DV
XA_W = XA_HEADS * XA_DH

OFF_ML_Q = 0
OFF_ML_K = OFF_ML_Q + ML_QK
OFF_ML_V = OFF_ML_K + ML_QK
OFF_ML_O = OFF_ML_V + ML_V
OFF_ML_Z = OFF_ML_O + ML_V
OFF_IF = OFF_ML_Z + ML_V
OFF_RT_Q = OFF_ML_Z + ML_V
OFF_RT_K = OFF_RT_Q + RET_QK
OFF_RT_V = OFF_RT_K + RET_QK
OFF_RT_Z = OFF_RT_V + RET_V
OFF_XA_Q = OFF_RT_Z + RET_V
OFF_XA_Z = OFF_XA_Q + XA_W
OFF_GATE = OFF_XA_Z + XA_W
N_MAIN = OFF_GATE + N_BRANCH * D_MODEL

CHUNK = 256
GATE_ROWS = 16
F_ROW = 8
CONV_PAD = 8
CONV_BLOCK = 64

VMEM_LIMIT = 56 * 1024 * 1024


def _params(sem):
    return pltpu.CompilerParams(dimension_semantics=sem, vmem_limit_bytes=VMEM_LIMIT)


def _sigmoid(x):
    return 0.5 * jnp.tanh(0.5 * x) + 0.5


def _rms(xf, g):
    return xf * lax.rsqrt(jnp.mean(xf * xf, axis=-1, keepdims=True) + EPS) * g


def _head_norm(t, g):
    mu = jnp.mean(t, axis=-1, keepdims=True)
    c = t - mu
    var = jnp.mean(c * c, axis=-1, keepdims=True)
    return c * lax.rsqrt(var + EPS) * g


def _dot(a, b):
    return jnp.dot(a, b, preferred_element_type=F32)


def _memkv_kernel(m_ref, g_ref, w_ref, o_ref):
    h = _rms(m_ref[...], g_ref[...])
    o_ref[...] = _dot(h.astype(BF16), w_ref[...]).astype(BF16)


def _memkv(mem2, g, w):
    rows = mem2.shape[0]
    tm = min(512, rows)
    return pl.pallas_call(
        _memkv_kernel,
        grid=(rows // tm,),
        in_specs=[pl.BlockSpec((tm, D_MODEL), lambda i: (i, 0)),
                  pl.BlockSpec((1, D_MODEL), lambda i: (0, 0)),
                  pl.BlockSpec((D_MODEL, 2 * XA_W), lambda i: (0, 0))],
        out_specs=pl.BlockSpec((tm, 2 * XA_W), lambda i: (i, 0)),
        out_shape=jax.ShapeDtypeStruct((rows, 2 * XA_W), BF16),
        compiler_params=_params(("arbitrary",)),
        name="memkv",
    )(mem2, g, w)


def _prep_kernel(w_ref, o_ref):
    o_ref[:, :OFF_IF] = w_ref[:, :OFF_IF].astype(BF16)
    o_ref[:, OFF_IF:] = w_ref[:, OFF_IF + 2 * ML_HEADS:].astype(BF16)


def _prep_w_in(w0):
    rows, n_in = w0.shape
    tr = 64
    return pl.pallas_call(
        _prep_kernel,
        grid=(rows // tr,),
        in_specs=[pl.BlockSpec((tr, n_in), lambda i: (i, 0))],
        out_specs=pl.BlockSpec((tr, N_MAIN), lambda i: (i, 0)),
        out_shape=jax.ShapeDtypeStruct((rows, N_MAIN), BF16),
        compiler_params=_params(("arbitrary",)),
        name="prep_w_in",
    )(w0)


def _inproj_kernel(x_ref, g_ref, w_ref, b_ref, wif_ref, bif_ref, p_ref, gate_ref, hb_ref):
    @pl.when(pl.program_id(1) == 0)
    def _():
        hb = _rms(x_ref[...], g_ref[...]).astype(BF16)
        hb_ref[...] = hb
        gt = lax.dot_general(wif_ref[...], hb, (((1,), (1,)), ((), ())), preferred_element_type=F32)
        gate_ref[...] = gt + bif_ref[...]

    p_ref[...] = (_dot(hb_ref[...], w_ref[...]) + b_ref[...]).astype(BF16)


def _inproj(x2, g, w_main, b_main, wif_t, bif_t, tm, tn):
    t = x2.shape[0]
    return pl.pallas_call(
        _inproj_kernel,
        grid=(t // tm, N_MAIN // tn),
        in_specs=[pl.BlockSpec((tm, D_MODEL), lambda i, j: (i, 0)),
                  pl.BlockSpec((1, D_MODEL), lambda i, j: (0, 0)),
                  pl.BlockSpec((D_MODEL, tn), lambda i, j: (0, j)),
                  pl.BlockSpec((1, tn), lambda i, j: (0, j)),
                  pl.BlockSpec((GATE_ROWS, D_MODEL), lambda i, j: (0, 0)),
                  pl.BlockSpec((GATE_ROWS, 1), lambda i, j: (0, 0))],
        out_specs=[pl.BlockSpec((tm, tn), lambda i, j: (i, j)),
                   pl.BlockSpec((GATE_ROWS, tm), lambda i, j: (0, i))],
        out_shape=[jax.ShapeDtypeStruct((t, N_MAIN), BF16),
                   jax.ShapeDtypeStruct((GATE_ROWS, t), F32)],
        scratch_shapes=[pltpu.VMEM((tm, D_MODEL), BF16)],
        compiler_params=_params(("arbitrary", "arbitrary")),
        name="inproj",
    )(x2, g, w_main, b_main, wif_t, bif_t)


def _rope_kernel(pos_ref, freq_ref, sign_ref, cos_ref, sin_ref):
    ang = pos_ref[...].astype(F32) * freq_ref[...]
    cos_ref[...] = jnp.cos(ang)
    sin_ref[...] = jnp.sin(ang) * sign_ref[...]


def _rope_tables(pos_col, freq2, sign2):
    t = pos_col.shape[0]
    tm = min(2048, t)
    return pl.pallas_call(
        _rope_kernel,
        grid=(t // tm,),
        in_specs=[pl.BlockSpec((tm, 1), lambda i: (i, 0)),
                  pl.BlockSpec((1, RET_DQK), lambda i: (0, 0)),
                  pl.BlockSpec((1, RET_DQK), lambda i: (0, 0))],
        out_specs=[pl.BlockSpec((tm, RET_DQK), lambda i: (i, 0)),
                   pl.BlockSpec((tm, RET_DQK), lambda i: (i, 0))],
        out_shape=[jax.ShapeDtypeStruct((t, RET_DQK), F32),
                   jax.ShapeDtypeStruct((t, RET_DQK), F32)],
        compiler_params=_params(("arbitrary",)),
        name="rope",
    )(pos_col, freq2, sign2)


def _split3(x):
    hi = x.astype(BF16)
    r1 = x - hi.astype(F32)
    mid = r1.astype(BF16)
    lo = (r1 - mid.astype(F32)).astype(BF16)
    return hi, mid, lo


def _mlstm_kernel(q_ref, k_ref, v_ref, o_ref, z_ref, gi_ref, gf_ref, cwq_ref, cwk_ref, cbq_ref, cbk_ref,
                  hg_ref, out_ref, pad_ref, qc_ref, kc_ref, b_ref, c_ref, n_ref, m_ref, *, seq):
    L = CHUNK
    nchunk = seq // L

    pad_ref[pl.ds(0, CONV_PAD), :] = jnp.zeros((CONV_PAD, 2 * ML_DQK), F32)
    pad_ref[pl.ds(CONV_PAD, seq), pl.ds(0, ML_DQK)] = q_ref[...].astype(F32)
    pad_ref[pl.ds(CONV_PAD, seq), pl.ds(ML_DQK, ML_DQK)] = k_ref[...].astype(F32)
    cw = jnp.concatenate([cwq_ref[...], cwk_ref[...]], axis=-1)
    cb = jnp.concatenate([cbq_ref[...], cbk_ref[...]], axis=-1)
    first = CONV_PAD - (ML_CONV - 1)

    def conv_block(i, carry):
        r = pl.multiple_of(i * CONV_BLOCK, CONV_BLOCK)
        xx = pad_ref[pl.ds(r, CONV_BLOCK + CONV_PAD), :]
        y = cb
        for j in range(ML_CONV):
            y = y + xx[first + j:first + j + CONV_BLOCK, :] * cw[j:j + 1, :]
        y = y * _sigmoid(y)
        qc_ref[pl.ds(r, CONV_BLOCK), :] = y[:, :ML_DQK] * (ML_DQK ** -0.5)
        kc_ref[pl.ds(r, CONV_BLOCK), :] = y[:, ML_DQK:]
        return carry

    lax.fori_loop(0, seq // CONV_BLOCK, conv_block, 0)

    f = gf_ref[0]
    lf = jnp.minimum(f, 0.0) - jnp.log1p(jnp.exp(-jnp.abs(f)))
    row = lax.broadcasted_iota(jnp.int32, (L, L), 0)
    col = lax.broadcasted_iota(jnp.int32, (L, L), 1)
    triu = (row <= col).astype(BF16)
    hi, mid, lo = _split3(lf)
    b_ref[...] = _dot(hi, triu) + _dot(mid, triu) + _dot(lo, triu)
    causal = row >= col

    c_ref[...] = jnp.zeros_like(c_ref)
    n_ref[...] = jnp.zeros_like(n_ref)
    m_ref[...] = jnp.zeros_like(m_ref)

    def chunk(c, carry):
        r0 = pl.multiple_of(c * L, L)
        rows = pl.ds(r0, L)
        b_row = b_ref[pl.ds(c, 1), :]
        i_row = gi_ref[0, pl.ds(c, 1), :]
        g = b_row[:, L - 1:L]
        br = jnp.broadcast_to(b_row, (L, L))
        bc = br.T
        b_col = bc[:, 0:1]
        log_d = jnp.where(causal, bc - br + i_row, -jnp.inf)
        m_prev = m_ref[...]
        log_inter = b_col + m_prev
        m_row = jnp.maximum(log_inter, jnp.max(log_d, axis=-1, keepdims=True))
        d = jnp.exp(log_d - m_row)
        qf = qc_ref[rows, :]
        kf = kc_ref[rows, :]
        qb = qf.astype(BF16)
        kt = kf.T
        vb = v_ref[rows, :]
        s = _dot(qb, kt.astype(BF16)) * d
        inter = jnp.exp(log_inter - m_row)
        num = _dot(s.astype(BF16), vb) + inter * _dot(qb, c_ref[...].astype(BF16))
        den = jnp.sum(s, axis=-1, keepdims=True) + inter * jnp.sum(qf * n_ref[...], axis=-1, keepdims=True)
        den = jnp.maximum(jnp.abs(den), jnp.exp(-m_row))
        hh = num * (1.0 / den)

        log_w = g - b_row + i_row
        m_new = jnp.maximum(g + m_prev, jnp.max(log_w, axis=-1, keepdims=True))
        w_row = jnp.exp(log_w - m_new)
        decay = jnp.exp(g + m_prev - m_new)
        c_ref[...] = decay * c_ref[...] + _dot((kt * w_row).astype(BF16), vb)
        w8 = jnp.broadcast_to(w_row, (8, L)).astype(BF16)
        n_ref[...] = decay * n_ref[...] + _dot(w8, kf.astype(BF16))[0:1, :]
        m_ref[...] = m_new

        y = _head_norm(hh, hg_ref[...])
        zf = z_ref[rows, :].astype(F32)
        y = y * _sigmoid(o_ref[rows, :].astype(F32)) * (zf * _sigmoid(zf))
        out_ref[rows, :] = y.astype(BF16)
        return carry

    lax.fori_loop(0, nchunk, chunk, 0)


def _mlstm(p, gates3, conv_w, conv_b, hnorm_g, batch, seq):
    t = batch * seq
    nchunk = seq // CHUNK
    qb, vb = ML_DQK, ML_DV

    def pspec(width, off):
        return pl.BlockSpec((seq, width), lambda b, h, off=off // width: (b, off + h))

    def gspec(row0):
        return pl.BlockSpec((1, nchunk, CHUNK), lambda b, h, row0=row0: (row0 + h, b, 0))

    kern = functools.partial(_mlstm_kernel, seq=seq)
    return pl.pallas_call(
        kern,
        grid=(batch, ML_HEADS),
        in_specs=[pspec(qb, OFF_ML_Q), pspec(qb, OFF_ML_K), pspec(vb, OFF_ML_V), pspec(vb, OFF_ML_O),
                  pspec(vb, OFF_ML_Z), gspec(0), gspec(F_ROW),
                  pl.BlockSpec((ML_CONV, qb), lambda b, h: (0, h)),
                  pl.BlockSpec((ML_CONV, qb), lambda b, h: (0, ML_HEADS + h)),
                  pl.BlockSpec((1, qb), lambda b, h: (0, h)),
                  pl.BlockSpec((1, qb), lambda b, h: (0, ML_HEADS + h)),
                  pl.BlockSpec((1, vb), lambda b, h: (0, h))],
        out_specs=pl.BlockSpec((seq, vb), lambda b, h: (b, h)),
        out_shape=jax.ShapeDtypeStruct((t, ML_V), BF16),
        scratch_shapes=[pltpu.VMEM((seq + CONV_PAD, 2 * qb), F32),
                        pltpu.VMEM((seq, qb), F32),
                        pltpu.VMEM((seq, qb), F32),
                        pltpu.VMEM((nchunk, CHUNK), F32),
                        pltpu.VMEM((qb, vb), F32),
                        pltpu.VMEM((1, qb), F32),
                        pltpu.VMEM((1, 1), F32)],
        compiler_params=_params(("arbitrary", "arbitrary")),
        name="mlstm",
    )(p, p, p, p, p, gates3, gates3, conv_w, conv_w, conv_b, conv_b, hnorm_g)


def _ret_kernel(lg_ref, q_ref, k_ref, v_ref, z_ref, cos_ref, sin_ref, hg_ref, out_ref, r_ref, *, seq):
    L = CHUNK
    nchunk = seq // L
    lg = lg_ref[pl.program_id(1)]
    row = lax.broadcasted_iota(jnp.int32, (L, L), 0)
    col = lax.broadcasted_iota(jnp.int32, (L, L), 1)
    intra = jnp.where(row >= col, jnp.exp((row - col).astype(F32) * lg), 0.0)
    pos_col = lax.broadcasted_iota(jnp.int32, (L, 1), 0).astype(F32)
    pos_row = lax.broadcasted_iota(jnp.int32, (1, L), 1).astype(F32)
    q_decay = jnp.exp((pos_col + 1.0) * lg)
    k_decay = jnp.exp((L - 1.0 - pos_row) * lg)
    chunk_decay = jnp.exp(jnp.full((1, 1), float(L), F32) * lg)
    r_ref[...] = jnp.zeros_like(r_ref)
    half = RET_DQK // 2

    def chunk(c, carry):
        r0 = pl.multiple_of(c * L, L)
        rows = pl.ds(r0, L)
        cs = cos_ref[rows, :]
        sn = sin_ref[rows, :]
        qf = q_ref[rows, :].astype(F32)
        kf = k_ref[rows, :].astype(F32)
        qr = qf * cs + pltpu.roll(qf, half, 1) * sn
        kr = (kf * cs + pltpu.roll(kf, half, 1) * sn) * (RET_DQK ** -0.5)
        qb = qr.astype(BF16)
        kt = kr.T
        vb = v_ref[rows, :]
        s = _dot(qb, kt.astype(BF16)) * intra
        o = _dot(s.astype(BF16), vb) + q_decay * _dot(qb, r_ref[...].astype(BF16))
        r_ref[...] = chunk_decay * r_ref[...] + _dot((kt * k_decay).astype(BF16), vb)
        y = _head_norm(o, hg_ref[...])
        zf = z_ref[rows, :].astype(F32)
        out_ref[rows, :] = (y * (zf * _sigmoid(zf))).astype(BF16)
        return carry

    lax.fori_loop(0, nchunk, chunk, 0, unroll=2)


def _retention(log_gamma, p, cos2, sin2, hnorm_g, batch, seq):
    t = batch * seq
    qb, vb = RET_DQK, RET_DV

    def pspec(width, off):
        return pl.BlockSpec((seq, width), lambda b, h, lg, off=off // width: (b, off + h))

    kern = functools.partial(_ret_kernel, seq=seq)
    grid_spec = pltpu.PrefetchScalarGridSpec(
        num_scalar_prefetch=1,
        grid=(batch, RET_HEADS),
        in_specs=[pspec(qb, OFF_RT_Q), pspec(qb, OFF_RT_K), pspec(vb, OFF_RT_V), pspec(vb, OFF_RT_Z),
                  pl.BlockSpec((seq, qb), lambda b, h, lg: (b, 0)),
                  pl.BlockSpec((seq, qb), lambda b, h, lg: (b, 0)),
                  pl.BlockSpec((1, vb), lambda b, h, lg: (0, h))],
        out_specs=pl.BlockSpec((seq, vb), lambda b, h, lg: (b, h)),
        scratch_shapes=[pltpu.VMEM((qb, vb), F32)],
    )
    return pl.pallas_call(
        kern,
        grid_spec=grid_spec,
        out_shape=jax.ShapeDtypeStruct((t, RET_V), BF16),
        compiler_params=_params(("arbitrary", "arbitrary")),
        name="retention",
    )(log_gamma, p, p, p, p, cos2, sin2, hnorm_g)


def _xattn_kernel(q_ref, z_ref, mk_ref, mv_ref, out_ref):
    sc = lax.dot_general(q_ref[...], mk_ref[...], (((1,), (1,)), ((), ())), preferred_element_type=F32)
    sc = sc * (XA_DH ** -0.5)
    e = jnp.exp(sc - jnp.max(sc, axis=-1, keepdims=True))
    o = _dot(e.astype(BF16), mv_ref[...]) * (1.0 / jnp.sum(e, axis=-1, keepdims=True))
    zf = z_ref[...].astype(F32)
    out_ref[...] = (o * (zf * _sigmoid(zf))).astype(BF16)


def _xattn(p, memkv, batch, seq):
    t = batch * seq
    ts = min(1024, seq)
    ns = seq // ts
    w = XA_DH
    return pl.pallas_call(
        _xattn_kernel,
        grid=(batch, XA_HEADS, ns),
        in_specs=[pl.BlockSpec((ts, w), lambda b, h, s: (b * ns + s, OFF_XA_Q // w + h)),
                  pl.BlockSpec((ts, w), lambda b, h, s: (b * ns + s, OFF_XA_Z // w + h)),
                  pl.BlockSpec((MEM_TOKENS, w), lambda b, h, s: (b, h)),
                  pl.BlockSpec((MEM_TOKENS, w), lambda b, h, s: (b, XA_HEADS + h))],
        out_specs=pl.BlockSpec((ts, w), lambda b, h, s: (b * ns + s, h)),
        out_shape=jax.ShapeDtypeStruct((t, XA_W), BF16),
        compiler_params=_params(("arbitrary", "arbitrary", "arbitrary")),
        name="xattn",
    )(p, p, memkv, memkv)


def _merge_kernel(ml_ref, rt_ref, xa_ref, wml_ref, wrt_ref, wxa_ref, g0_ref, g1_ref, g2_ref, o_ref):
    acc = _sigmoid(g0_ref[...].astype(F32)) * _dot(ml_ref[...], wml_ref[...])
    acc = acc + _sigmoid(g1_ref[...].astype(F32)) * _dot(rt_ref[...], wrt_ref[...])
    acc = acc + _sigmoid(g2_ref[...].astype(F32)) * _dot(xa_ref[...], wxa_ref[...])
    o_ref[...] = acc.astype(BF16)


def _merge(ml_out, rt_out, xa_out, w_ml, w_rt, w_xa, p, tm, tn):
    t = ml_out.shape[0]
    nj = D_MODEL // tn

    def gspec(br):
        return pl.BlockSpec((tm, tn), lambda i, j, o=(OFF_GATE + br * D_MODEL) // tn: (i, o + j))

    return pl.pallas_call(
        _merge_kernel,
        grid=(t // tm, nj),
        in_specs=[pl.BlockSpec((tm, ML_V), lambda i, j: (i, 0)),
                  pl.BlockSpec((tm, RET_V), lambda i, j: (i, 0)),
                  pl.BlockSpec((tm, XA_W), lambda i, j: (i, 0)),
                  pl.BlockSpec((ML_V, tn), lambda i, j: (0, j)),
                  pl.BlockSpec((RET_V, tn), lambda i, j: (0, j)),
                  pl.BlockSpec((XA_W, tn), lambda i, j: (0, j)),
                  gspec(0), gspec(1), gspec(2)],
        out_specs=pl.BlockSpec((tm, tn), lambda i, j: (i, j)),
        out_shape=jax.ShapeDtypeStruct((t, D_MODEL), BF16),
        compiler_params=_params(("arbitrary", "arbitrary")),
        name="merge",
    )(ml_out, rt_out, xa_out, w_ml, w_rt, w_xa, p, p, p)


def _outproj_kernel(x_ref, m_ref, w_ref, g_ref, o_ref):
    y = x_ref[...] + _dot(m_ref[...], w_ref[...])
    o_ref[...] = _rms(y, g_ref[...])


def _outproj(x2, merged, w_out, final_g, tm):
    t = x2.shape[0]
    return pl.pallas_call(
        _outproj_kernel,
        grid=(t // tm,),
        in_specs=[pl.BlockSpec((tm, D_MODEL), lambda i: (i, 0)),
                  pl.BlockSpec((tm, D_MODEL), lambda i: (i, 0)),
                  pl.BlockSpec((D_MODEL, D_MODEL), lambda i: (0, 0)),
                  pl.BlockSpec((1, D_MODEL), lambda i: (0, 0))],
        out_specs=pl.BlockSpec((tm, D_MODEL), lambda i: (i, 0)),
        out_shape=jax.ShapeDtypeStruct((t, D_MODEL), F32),
        compiler_params=_params(("arbitrary",)),
        name="outproj",
    )(x2, merged, w_out, final_g)


def kernel(x, mem, positions, ln_g, mem_ln_g, w_in, b_in, conv_w, conv_b, ml_hnorm_g, ret_hnorm_g, w_mem_kv,
           w_br_ml, w_br_ret, w_br_xa, w_out, final_g):
    batch, seq, _ = x.shape
    assert seq % CHUNK == 0 and ln_g.shape[0] == 1
    t = batch * seq
    x2 = x.reshape(t, D_MODEL)

    w0, b0 = w_in[0], b_in[0]
    w_main = _prep_w_in(w0)
    b_main = jnp.concatenate([b0[:OFF_IF], b0[OFF_IF + 2 * ML_HEADS:]])[None, :]
    wif_t = jnp.zeros((GATE_ROWS, D_MODEL), F32)
    wif_t = wif_t.at[0:ML_HEADS].set(w0[:, OFF_IF:OFF_IF + ML_HEADS].T)
    wif_t = wif_t.at[F_ROW:F_ROW + ML_HEADS].set(w0[:, OFF_IF + ML_HEADS:OFF_IF + 2 * ML_HEADS].T).astype(BF16)
    bif_t = jnp.zeros((GATE_ROWS, 1), F32)
    bif_t = bif_t.at[0:ML_HEADS, 0].set(b0[OFF_IF:OFF_IF + ML_HEADS])
    bif_t = bif_t.at[F_ROW:F_ROW + ML_HEADS, 0].set(b0[OFF_IF + ML_HEADS:OFF_IF + 2 * ML_HEADS])

    memkv = _memkv(mem.reshape(batch * MEM_TOKENS, D_MODEL), mem_ln_g[0][None, :], w_mem_kv[0].astype(BF16))

    tm = min(1024, t)
    p, gates_t = _inproj(x2, ln_g[0][None, :], w_main, b_main, wif_t, bif_t, tm, 512)
    gates3 = gates_t.reshape(GATE_ROWS, t // CHUNK, CHUNK)

    half = RET_DQK // 2
    freqs = ROPE_BASE ** (-jnp.arange(half, dtype=F32) / half)
    freq2 = jnp.concatenate([freqs, freqs])[None, :]
    sign2 = jnp.concatenate([-jnp.ones((half,), F32), jnp.ones((half,), F32)])[None, :]
    cos2, sin2 = _rope_tables(positions.reshape(t, 1), freq2, sign2)
    log_gamma = jnp.asarray(np.log(1.0 - 2.0 ** (-5.0 - np.arange(RET_HEADS))), dtype=F32)

    ml_out = _mlstm(p, gates3, conv_w[0], conv_b[0][None, :], ml_hnorm_g[0][None, :], batch, seq)
    rt_out = _retention(log_gamma, p, cos2, sin2, ret_hnorm_g[0][None, :], batch, seq)
    xa_out = _xattn(p, memkv, batch, seq)

    merged = _merge(ml_out, rt_out, xa_out, w_br_ml[0].astype(BF16), w_br_ret[0].astype(BF16),
                    w_br_xa[0].astype(BF16), p, tm, 512)
    out = _outproj(x2, merged, w_out[0].astype(BF16), final_g[None, :], min(512, t))
    return out.reshape(batch, seq, D_MODEL)
```

```python
import functools

import numpy as np
import jax
import jax.numpy as jnp
from jax import lax
from jax.experimental import pallas as pl
from jax.experimental.pallas import tpu as pltpu

F32 = jnp.float32
BF16 = jnp.bfloat16

D_MODEL = 2048
MEM_TOKENS = 256
ML_HEADS = 6
ML_DQK = 128
ML_DV = 256
ML_CONV = 4
RET_HEADS = 6
RET_DQK = 128
RET_DV = 256
XA_HEADS = 4
XA_DH = 256
ROPE_BASE = 10000.0
EPS = 1e-6
N_BRANCH = 3

ML_QK = ML_HEADS * ML_DQK
ML_V = ML_HEADS * ML_DV
RET_QK = RET_HEADS * RET_DQK
RET_V = RET_HEADS * RET_DV
XA_W = XA_HEADS * XA_DH

OFF_ML_Q = 0
OFF_ML_K = OFF_ML_Q + ML_QK
OFF_ML_V = OFF_ML_K + ML_QK
OFF_ML_O = OFF_ML_V + ML_V
OFF_ML_Z = OFF_ML_O + ML_V
OFF_IF = OFF_ML_Z + ML_V
OFF_RT_Q = OFF_ML_Z + ML_V
OFF_RT_K = OFF_RT_Q + RET_QK
OFF_RT_V = OFF_RT_K + RET_QK
OFF_RT_Z = OFF_RT_V + RET_V
OFF_XA_Q = OFF_RT_Z + RET_V
OFF_XA_Z = OFF_XA_Q + XA_W
OFF_GATE = OFF_XA_Z + XA_W
N_MAIN = OFF_GATE + N_BRANCH * D_MODEL

CHUNK = 256
GATE_ROWS = 16
F_ROW = 8
CONV_PAD = 8
CONV_BLOCK = 64

LANES = 128
VMEM_LIMIT = 56 * 1024 * 1024


def _params(sem):
    return pltpu.CompilerParams(dimension_semantics=sem, vmem_limit_bytes=VMEM_LIMIT)


def _sigmoid(x):
    return 0.5 * jnp.tanh(0.5 * x) + 0.5


def _rms(xf, g):
    return xf * lax.rsqrt(jnp.mean(xf * xf, axis=-1, keepdims=True) + EPS) * g


def _head_norm(t, g):
    mu = jnp.mean(t, axis=-1, keepdims=True)
    c = t - mu
    var = jnp.mean(c * c, axis=-1, keepdims=True)
    return c * lax.rsqrt(var + EPS) * g


def _dot(a, b):
    return jnp.dot(a, b, preferred_element_type=F32)


def _memkv_kernel(m_ref, g_ref, w_ref, o_ref):
    h = _rms(m_ref[...], g_ref[...])
    o_ref[...] = _dot(h.astype(BF16), w_ref[...]).astype(BF16)


def _memkv(mem2, g, w):
    rows = mem2.shape[0]
    tm = min(512, rows)
    return pl.pallas_call(
        _memkv_kernel,
        grid=(rows // tm,),
        in_specs=[pl.BlockSpec((tm, D_MODEL), lambda i: (i, 0)),
                  pl.BlockSpec((1, D_MODEL), lambda i: (0, 0)),
                  pl.BlockSpec((D_MODEL, 2 * XA_W), lambda i: (0, 0))],
        out_specs=pl.BlockSpec((tm, 2 * XA_W), lambda i: (i, 0)),
        out_shape=jax.ShapeDtypeStruct((rows, 2 * XA_W), BF16),
        compiler_params=_params(("arbitrary",)),
        name="memkv",
    )(mem2, g, w)


def _prep_kernel(w_ref, o_ref, gate_ref):
    o_ref[:, :OFF_IF] = w_ref[:, :OFF_IF].astype(BF16)
    o_ref[:, OFF_IF:] = w_ref[:, OFF_IF + 2 * ML_HEADS:].astype(BF16)
    gate_ref[...] = w_ref[:, OFF_IF:OFF_IF + LANES]


def _prep_w_in(w0):
    rows, n_in = w0.shape
    tr = 64
    return pl.pallas_call(
        _prep_kernel,
        grid=(rows // tr,),
        in_specs=[pl.BlockSpec((tr, n_in), lambda i: (i, 0))],
        out_specs=[pl.BlockSpec((tr, N_MAIN), lambda i: (i, 0)),
                   pl.BlockSpec((tr, LANES), lambda i: (i, 0))],
        out_shape=[jax.ShapeDtypeStruct((rows, N_MAIN), BF16),
                   jax.ShapeDtypeStruct((rows, LANES), F32)],
        compiler_params=_params(("arbitrary",)),
        name="prep_w_in",
    )(w0)


def _inproj_kernel(x_ref, g_ref, w_ref, b_ref, wif_ref, bif_ref, p_ref, gate_ref, hb_ref, *, nsplit):
    j = pl.program_id(1)
    tx = x_ref.shape[0]

    @pl.when(j < nsplit)
    def _():
        hb = _rms(x_ref[...], g_ref[...]).astype(BF16)
        r0 = pl.multiple_of(j * tx, tx)
        hb_ref[pl.ds(r0, tx), :] = hb
        gt = lax.dot_general(wif_ref[...], hb, (((1,), (1,)), ((), ())), preferred_element_type=F32)
        gate_ref[:, pl.ds(r0, tx)] = gt + bif_ref[...]

    @pl.when(j >= nsplit)
    def _():
        p_ref[...] = (_dot(hb_ref[...], w_ref[...]) + b_ref[...]).astype(BF16)


def _inproj(x2, g, w_main, b_main, wif_t, bif_t, tm, tn, nsplit):
    t = x2.shape[0]
    tx = tm // nsplit

    def col(j):
        return jnp.maximum(j - nsplit, 0)

    return pl.pallas_call(
        functools.partial(_inproj_kernel, nsplit=nsplit),
        grid=(t // tm, nsplit + N_MAIN // tn),
        in_specs=[pl.BlockSpec((tx, D_MODEL), lambda i, j: (i * nsplit + jnp.minimum(j, nsplit - 1), 0)),
                  pl.BlockSpec((1, D_MODEL), lambda i, j: (0, 0)),
                  pl.BlockSpec((D_MODEL, tn), lambda i, j: (0, col(j))),
                  pl.BlockSpec((1, tn), lambda i, j: (0, col(j))),
                  pl.BlockSpec((GATE_ROWS, D_MODEL), lambda i, j: (0, 0)),
                  pl.BlockSpec((GATE_ROWS, 1), lambda i, j: (0, 0))],
        out_specs=[pl.BlockSpec((tm, tn), lambda i, j: (i, col(j))),
                   pl.BlockSpec((GATE_ROWS, tm), lambda i, j: (0, i))],
        out_shape=[jax.ShapeDtypeStruct((t, N_MAIN), BF16),
                   jax.ShapeDtypeStruct((GATE_ROWS, t), F32)],
        scratch_shapes=[pltpu.VMEM((tm, D_MODEL), BF16)],
        compiler_params=_params(("arbitrary", "arbitrary")),
        name="inproj",
    )(x2, g, w_main, b_main, wif_t, bif_t)


def _rope_kernel(pos_ref, freq_ref, sign_ref, cos_ref, sin_ref):
    ang = pos_ref[...].astype(F32) * freq_ref[...]
    cos_ref[...] = jnp.cos(ang)
    sin_ref[...] = jnp.sin(ang) * sign_ref[...]


def _rope_tables(pos_col, freq2, sign2):
    t = pos_col.shape[0]
    tm = min(2048, t)
    return pl.pallas_call(
        _rope_kernel,
        grid=(t // tm,),
        in_specs=[pl.BlockSpec((tm, 1), lambda i: (i, 0)),
                  pl.BlockSpec((1, RET_DQK), lambda i: (0, 0)),
                  pl.BlockSpec((1, RET_DQK), lambda i: (0, 0))],
        out_specs=[pl.BlockSpec((tm, RET_DQK), lambda i: (i, 0)),
                   pl.BlockSpec((tm, RET_DQK), lambda i: (i, 0))],
        out_shape=[jax.ShapeDtypeStruct((t, RET_DQK), F32),
                   jax.ShapeDtypeStruct((t, RET_DQK), F32)],
        compiler_params=_params(("arbitrary",)),
        name="rope",
    )(pos_col, freq2, sign2)


def _split3(x):
    hi = x.astype(BF16)
    r1 = x - hi.astype(F32)
    mid = r1.astype(BF16)
    lo = (r1 - mid.astype(F32)).astype(BF16)
    return hi, mid, lo


def _mlstm_kernel(q_ref, k_ref, v_ref, o_ref, z_ref, gi_ref, gf_ref, cwq_ref, cwk_ref, cbq_ref, cbk_ref,
                  hg_ref, out_ref, pad_ref, qc_ref, kc_ref, b_ref, a_ref, w_ref, mp_ref, mn_ref, dec_ref,
                  c_ref, n_ref, *, seq):
    L = CHUNK
    nchunk = seq // L

    pad_ref[pl.ds(0, CONV_PAD), :] = jnp.zeros((CONV_PAD, 2 * ML_DQK), F32)
    pad_ref[pl.ds(CONV_PAD, seq), pl.ds(0, ML_DQK)] = q_ref[...].astype(F32)
    pad_ref[pl.ds(CONV_PAD, seq), pl.ds(ML_DQK, ML_DQK)] = k_ref[...].astype(F32)
    cw = jnp.concatenate([cwq_ref[...], cwk_ref[...]], axis=-1)
    cb = jnp.concatenate([cbq_ref[...], cbk_ref[...]], axis=-1)
    first = CONV_PAD - (ML_CONV - 1)

    def conv_block(i, carry):
        r = pl.multiple_of(i * CONV_BLOCK, CONV_BLOCK)
        xx = pad_ref[pl.ds(r, CONV_BLOCK + CONV_PAD), :]
        y = cb
        for j in range(ML_CONV):
            y = y + xx[first + j:first + j + CONV_BLOCK, :] * cw[j:j + 1, :]
        y = y * _sigmoid(y)
        qc_ref[pl.ds(r, CONV_BLOCK), :] = y[:, :ML_DQK] * (ML_DQK ** -0.5)
        kc_ref[pl.ds(r, CONV_BLOCK), :] = y[:, ML_DQK:]
        return carry

    lax.fori_loop(0, seq // CONV_BLOCK, conv_block, 0)

    f = gf_ref[0]
    lf = jnp.minimum(f, 0.0) - jnp.log1p(jnp.exp(-jnp.abs(f)))
    row = lax.broadcasted_iota(jnp.int32, (L, L), 0)
    col = lax.broadcasted_iota(jnp.int32, (L, L), 1)
    triu = (row <= col).astype(BF16)
    hi, mid, lo = _split3(lf)
    bb = _dot(hi, triu) + _dot(mid, triu) + _dot(lo, triu)
    causal = row >= col
    diag = row == col

    gi = gi_ref[0]
    g_col = bb[:, L - 1:L]
    log_w = g_col - bb + gi
    w_max = jnp.max(log_w, axis=-1, keepdims=True)
    m = jnp.zeros((1, 1), F32)
    for c in range(nchunk):
        mp_ref[c:c + 1, :] = m
        m = jnp.maximum(g_col[c:c + 1, :] + m, w_max[c:c + 1, :])
        mn_ref[c:c + 1, :] = m
    m_prev_all = mp_ref[...]
    m_new_all = mn_ref[...]
    b_ref[...] = bb
    a_ref[...] = gi - bb
    w_ref[...] = jnp.exp(log_w - m_new_all)
    dec_ref[...] = jnp.exp(g_col + m_prev_all - m_new_all)

    c_ref[...] = jnp.zeros_like(c_ref)
    n_ref[...] = jnp.zeros_like(n_ref)
    hg_half = 0.5 * hg_ref[...]

    def chunk(c, carry):
        r0 = pl.multiple_of(c * L, L)
        rows = pl.ds(r0, L)
        one = pl.ds(c, 1)
        b_row = b_ref[one, :]
        a_row = a_ref[one, :]
        w_row = w_ref[one, :]
        m_prev = mp_ref[one, :]
        decay = dec_ref[one, :]
        a_mat = jnp.where(causal, a_row, -jnp.inf)
        u = jnp.maximum(m_prev, jnp.max(a_mat, axis=-1, keepdims=True))
        d = jnp.exp(a_mat - u)
        b_col = jnp.sum(jnp.where(diag, b_row, 0.0), axis=-1, keepdims=True)
        inter = jnp.exp(m_prev - u)
        qf = qc_ref[rows, :]
        kf = kc_ref[rows, :]
        qb = qf.astype(BF16)
        kt = kf.T
        vb = v_ref[rows, :]
        s = _dot(qb, kt.astype(BF16)) * d
        num = _dot(s.astype(BF16), vb) + inter * _dot(qb, c_ref[...].astype(BF16))
        den = jnp.sum(s, axis=-1, keepdims=True) + inter * jnp.sum(qf * n_ref[...], axis=-1, keepdims=True)
        den = jnp.maximum(jnp.abs(den), jnp.exp(-(b_col + u)))
        hh = num * (1.0 / den)

        c_ref[...] = decay * c_ref[...] + _dot((kt * w_row).astype(BF16), vb)
        w8 = jnp.broadcast_to(w_row, (8, L)).astype(BF16)
        n_ref[...] = decay * n_ref[...] + _dot(w8, kf.astype(BF16))[0:1, :]

        y = _head_norm(hh, hg_half)
        oh = 0.5 * o_ref[rows, :].astype(F32)
        zh = 0.5 * z_ref[rows, :].astype(F32)
        y = y * (1.0 + jnp.tanh(oh)) * (zh * (1.0 + jnp.tanh(zh)))
        out_ref[rows, :] = y.astype(BF16)
        return carry

    lax.fori_loop(0, nchunk, chunk, 0, unroll=2)


def _mlstm(p, gates3, conv_w, conv_b, hnorm_g, batch, seq):
    t = batch * seq
    nchunk = seq // CHUNK
    qb, vb = ML_DQK, ML_DV

    def pspec(width, off):
        return pl.BlockSpec((seq, width), lambda b, h, off=off // width: (b, off + h))

    def gspec(row0):
        return pl.BlockSpec((1, nchunk, CHUNK), lambda b, h, row0=row0: (row0 + h, b, 0))

    kern = functools.partial(_mlstm_kernel, seq=seq)
    return pl.pallas_call(
        kern,
        grid=(batch, ML_HEADS),
        in_specs=[pspec(qb, OFF_ML_Q), pspec(qb, OFF_ML_K), pspec(vb, OFF_ML_V), pspec(vb, OFF_ML_O),
                  pspec(vb, OFF_ML_Z), gspec(0), gspec(F_ROW),
                  pl.BlockSpec((ML_CONV, qb), lambda b, h: (0, h)),
                  pl.BlockSpec((ML_CONV, qb), lambda b, h: (0, ML_HEADS + h)),
                  pl.BlockSpec((1, qb), lambda b, h: (0, h)),
                  pl.BlockSpec((1, qb), lambda b, h: (0, ML_HEADS + h)),
                  pl.BlockSpec((1, vb), lambda b, h: (0, h))],
        out_specs=pl.BlockSpec((seq, vb), lambda b, h: (b, h)),
        out_shape=jax.ShapeDtypeStruct((t, ML_V), BF16),
        scratch_shapes=[pltpu.VMEM((seq + CONV_PAD, 2 * qb), F32),
                        pltpu.VMEM((seq, qb), F32),
                        pltpu.VMEM((seq, qb), F32),
                        pltpu.VMEM((nchunk, CHUNK), F32),
                        pltpu.VMEM((nchunk, CHUNK), F32),
                        pltpu.VMEM((nchunk, CHUNK), F32),
                        pltpu.VMEM((nchunk, 1), F32),
                        pltpu.VMEM((nchunk, 1), F32),
                        pltpu.VMEM((nchunk, 1), F32),
                        pltpu.VMEM((qb, vb), F32),
                        pltpu.VMEM((1, qb), F32)],
        compiler_params=_params(("arbitrary", "arbitrary")),
        name="mlstm",
    )(p, p, p, p, p, gates3, gates3, conv_w, conv_w, conv_b, conv_b, hnorm_g)


def _ret_kernel(lg_ref, q_ref, k_ref, v_ref, z_ref, cos_ref, sin_ref, hg_ref, out_ref, r_ref, *, seq):
    L = CHUNK
    nchunk = seq // L
    lg = lg_ref[pl.program_id(1)]
    row = lax.broadcasted_iota(jnp.int32, (L, L), 0)
    col = lax.broadcasted_iota(jnp.int32, (L, L), 1)
    intra = jnp.where(row >= col, jnp.exp((row - col).astype(F32) * lg), 0.0)
    pos_col = lax.broadcasted_iota(jnp.int32, (L, 1), 0).astype(F32)
    pos_row = lax.broadcasted_iota(jnp.int32, (1, L), 1).astype(F32)
    q_decay = jnp.exp((pos_col + 1.0) * lg)
    k_decay = jnp.exp((L - 1.0 - pos_row) * lg)
    chunk_decay = jnp.exp(jnp.full((1, 1), float(L), F32) * lg)
    r_ref[...] = jnp.zeros_like(r_ref)
    half = RET_DQK // 2

    def chunk(c, carry):
        r0 = pl.multiple_of(c * L, L)
        rows = pl.ds(r0, L)
        cs = cos_ref[rows, :]
        sn = sin_ref[rows, :]
        qf = q_ref[rows, :].astype(F32)
        kf = k_ref[rows, :].astype(F32)
        qr = qf * cs + pltpu.roll(qf, half, 1) * sn
        kr = (kf * cs + pltpu.roll(kf, half, 1) * sn) * (RET_DQK ** -0.5)
        qb = qr.astype(BF16)
        kt = kr.T
        vb = v_ref[rows, :]
        s = _dot(qb, kt.astype(BF16)) * intra
        o = _dot(s.astype(BF16), vb) + q_decay * _dot(qb, r_ref[...].astype(BF16))
        r_ref[...] = chunk_decay * r_ref[...] + _dot((kt * k_decay).astype(BF16), vb)
        y = _head_norm(o, hg_ref[...])
        zf = z_ref[rows, :].astype(F32)
        out_ref[rows, :] = (y * (zf * _sigmoid(zf))).astype(BF16)
        return carry

    lax.fori_loop(0, nchunk, chunk, 0, unroll=2)


def _retention(log_gamma, p, cos2, sin2, hnorm_g, batch, seq):
    t = batch * seq
    qb, vb = RET_DQK, RET_DV

    def pspec(width, off):
        return pl.BlockSpec((seq, width), lambda b, h, lg, off=off // width: (b, off + h))

    kern = functools.partial(_ret_kernel, seq=seq)
    grid_spec = pltpu.PrefetchScalarGridSpec(
        num_scalar_prefetch=1,
        grid=(batch, RET_HEADS),
        in_specs=[pspec(qb, OFF_RT_Q), pspec(qb, OFF_RT_K), pspec(vb, OFF_RT_V), pspec(vb, OFF_RT_Z),
                  pl.BlockSpec((seq, qb), lambda b, h, lg: (b, 0)),
                  pl.BlockSpec((seq, qb), lambda b, h, lg: (b, 0)),
                  pl.BlockSpec((1, vb), lambda b, h, lg: (0, h))],
        out_specs=pl.BlockSpec((seq, vb), lambda b, h, lg: (b, h)),
        scratch_shapes=[pltpu.VMEM((qb, vb), F32)],
    )
    return pl.pallas_call(
        kern,
        grid_spec=grid_spec,
        out_shape=jax.ShapeDtypeStruct((t, RET_V), BF16),
        compiler_params=_params(("arbitrary", "arbitrary")),
        name="retention",
    )(log_gamma, p, p, p, p, cos2, sin2, hnorm_g)


def _xattn_kernel(q_ref, z_ref, mk_ref, mv_ref, out_ref):
    sc = lax.dot_general(q_ref[...], mk_ref[...], (((1,), (1,)), ((), ())), preferred_element_type=F32)
    sc = sc * (XA_DH ** -0.5)
    e = jnp.exp(sc - jnp.max(sc, axis=-1, keepdims=True))
    o = _dot(e.astype(BF16), mv_ref[...]) * (1.0 / jnp.sum(e, axis=-1, keepdims=True))
    zf = z_ref[...].astype(F32)
    out_ref[...] = (o * (zf * _sigmoid(zf))).astype(BF16)


def _xattn(p, memkv, batch, seq):
    t = batch * seq
    ts = min(1024, seq)
    ns = seq // ts
    w = XA_DH
    return pl.pallas_call(
        _xattn_kernel,
        grid=(batch, XA_HEADS, ns),
        in_specs=[pl.BlockSpec((ts, w), lambda b, h, s: (b * ns + s, OFF_XA_Q // w + h)),
                  pl.BlockSpec((ts, w), lambda b, h, s: (b * ns + s, OFF_XA_Z // w + h)),
                  pl.BlockSpec((MEM_TOKENS, w), lambda b, h, s: (b, h)),
                  pl.BlockSpec((MEM_TOKENS, w), lambda b, h, s: (b, XA_HEADS + h))],
        out_specs=pl.BlockSpec((ts, w), lambda b, h, s: (b * ns + s, h)),
        out_shape=jax.ShapeDtypeStruct((t, XA_W), BF16),
        compiler_params=_params(("arbitrary", "arbitrary", "arbitrary")),
        name="xattn",
    )(p, p, memkv, memkv)


def _merge_kernel(ml_ref, rt_ref, xa_ref, wml_ref, wrt_ref, wxa_ref, g0_ref, g1_ref, g2_ref, o_ref):
    acc = _sigmoid(g0_ref[...].astype(F32)) * _dot(ml_ref[...], wml_ref[...])
    acc = acc + _sigmoid(g1_ref[...].astype(F32)) * _dot(rt_ref[...], wrt_ref[...])
    acc = acc + _sigmoid(g2_ref[...].astype(F32)) * _dot(xa_ref[...], wxa_ref[...])
    o_ref[...] = acc.astype(BF16)


def _merge(ml_out, rt_out, xa_out, w_ml, w_rt, w_xa, p, tm, tn):
    t = ml_out.shape[0]
    nj = D_MODEL // tn

    def gspec(br):
        return pl.BlockSpec((tm, tn), lambda i, j, o=(OFF_GATE + br * D_MODEL) // tn: (i, o + j))

    return pl.pallas_call(
        _merge_kernel,
        grid=(t // tm, nj),
        in_specs=[pl.BlockSpec((tm, ML_V), lambda i, j: (i, 0)),
                  pl.BlockSpec((tm, RET_V), lambda i, j: (i, 0)),
                  pl.BlockSpec((tm, XA_W), lambda i, j: (i, 0)),
                  pl.BlockSpec((ML_V, tn), lambda i, j: (0, j)),
                  pl.BlockSpec((RET_V, tn), lambda i, j: (0, j)),
                  pl.BlockSpec((XA_W, tn), lambda i, j: (0, j)),
                  gspec(0), gspec(1), gspec(2)],
        out_specs=pl.BlockSpec((tm, tn), lambda i, j: (i, j)),
        out_shape=jax.ShapeDtypeStruct((t, D_MODEL), BF16),
        compiler_params=_params(("arbitrary", "arbitrary")),
        name="merge",
    )(ml_out, rt_out, xa_out, w_ml, w_rt, w_xa, p, p, p)


def _outproj_kernel(x_ref, m_ref, w_ref, g_ref, o_ref):
    y = x_ref[...] + _dot(m_ref[...], w_ref[...])
    o_ref[...] = _rms(y, g_ref[...])


def _outproj(x2, merged, w_out, final_g, tm):
    t = x2.shape[0]
    return pl.pallas_call(
        _outproj_kernel,
        grid=(t // tm,),
        in_specs=[pl.BlockSpec((tm, D_MODEL), lambda i: (i, 0)),
                  pl.BlockSpec((tm, D_MODEL), lambda i: (i, 0)),
                  pl.BlockSpec((D_MODEL, D_MODEL), lambda i: (0, 0)),
                  pl.BlockSpec((1, D_MODEL), lambda i: (0, 0))],
        out_specs=pl.BlockSpec((tm, D_MODEL), lambda i: (i, 0)),
        out_shape=jax.ShapeDtypeStruct((t, D_MODEL), F32),
        compiler_params=_params(("arbitrary",)),
        name="outproj",
    )(x2, merged, w_out, final_g)


def kernel(x, mem, positions, ln_g, mem_ln_g, w_in, b_in, conv_w, conv_b, ml_hnorm_g, ret_hnorm_g, w_mem_kv,
           w_br_ml, w_br_ret, w_br_xa, w_out, final_g):
    batch, seq, _ = x.shape
    assert seq % CHUNK == 0 and ln_g.shape[0] == 1
    t = batch * seq
    x2 = x.reshape(t, D_MODEL)

    w0, b0 = w_in[0], b_in[0]
    w_main, w_gate = _prep_w_in(w0)
    b_main = jnp.concatenate([b0[:OFF_IF], b0[OFF_IF + 2 * ML_HEADS:]])[None, :]
    wif_t = jnp.zeros((GATE_ROWS, D_MODEL), F32)
    wif_t = wif_t.at[0:ML_HEADS].set(w_gate[:, 0:ML_HEADS].T)
    wif_t = wif_t.at[F_ROW:F_ROW + ML_HEADS].set(w_gate[:, ML_HEADS:2 * ML_HEADS].T).astype(BF16)
    bif_t = jnp.zeros((GATE_ROWS, 1), F32)
    bif_t = bif_t.at[0:ML_HEADS, 0].set(b0[OFF_IF:OFF_IF + ML_HEADS])
    bif_t = bif_t.at[F_ROW:F_ROW + ML_HEADS, 0].set(b0[OFF_IF + ML_HEADS:OFF_IF + 2 * ML_HEADS])

    memkv = _memkv(mem.reshape(batch * MEM_TOKENS, D_MODEL), mem_ln_g[0][None, :], w_mem_kv[0].astype(BF16))

    tm = min(1024, t)
    p, gates_t = _inproj(x2, ln_g[0][None, :], w_main, b_main, wif_t, bif_t, min(2048, t), 512, 2)
    gates3 = gates_t.reshape(GATE_ROWS, t // CHUNK, CHUNK)

    half = RET_DQK // 2
    freqs = ROPE_BASE ** (-jnp.arange(half, dtype=F32) / half)
    freq2 = jnp.concatenate([freqs, freqs])[None, :]
    sign2 = jnp.concatenate([-jnp.ones((half,), F32), jnp.ones((half,), F32)])[None, :]
    cos2, sin2 = _rope_tables(positions.reshape(t, 1), freq2, sign2)
    log_gamma = jnp.asarray(np.log(1.0 - 2.0 ** (-5.0 - np.arange(RET_HEADS))), dtype=F32)

    ml_out = _mlstm(p, gates3, conv_w[0], conv_b[0][None, :], ml_hnorm_g[0][None, :], batch, seq)
    rt_out = _retention(log_gamma, p, cos2, sin2, ret_hnorm_g[0][None, :], batch, seq)
    xa_out = _xattn(p, memkv, batch, seq)

    merged = _merge(ml_out, rt_out, xa_out, w_br_ml[0].astype(BF16), w_br_ret[0].astype(BF16),
                    w_br_xa[0].astype(BF16), p, tm, 512)
    out = _outproj(x2, merged, w_out[0].astype(BF16), final_g[None, :], min(512, t))
    return out.reshape(batch, seq, D_MODEL)
```

```python
import functools

import numpy as np
import jax
import jax.numpy as jnp
from jax import lax
from jax.experimental import pallas as pl
from jax.experimental.pallas import tpu as pltpu

F32 = jnp.float32
BF16 = jnp.bfloat16

D_MODEL = 2048
MEM_TOKENS = 256
ML_HEADS = 6
ML_DQK = 128
ML_DV = 256
ML_CONV = 4
RET_HEADS = 6
RET_DQK = 128
RET_DV = 256
XA_HEADS = 4
XA_DH = 256
ROPE_BASE = 10000.0
EPS = 1e-6
N_BRANCH = 3

ML_QK = ML_HEADS * ML_DQK
ML_V = ML_HEADS * ML_DV
RET_QK = RET_HEADS * RET_DQK
RET_V = RET_HEADS * RET_DV
XA_W = XA_HEADS * XA_DH

OFF_ML_Q = 0
OFF_ML_K = OFF_ML_Q + ML_QK
OFF_ML_V = OFF_ML_K + ML_QK
OFF_ML_O = OFF_ML_V + ML_V
OFF_ML_Z = OFF_ML_O + ML_V
OFF_IF = OFF_ML_Z + ML_V
OFF_RT_Q = OFF_ML_Z + ML_V
OFF_RT_K = OFF_RT_Q + RET_QK
OFF_RT_V = OFF_RT_K + RET_QK
OFF_RT_Z = OFF_RT_V + RET_V
OFF_XA_Q = OFF_RT_Z + RET_V
OFF_XA_Z = OFF_XA_Q + XA_W
OFF_GATE = OFF_XA_Z + XA_W
N_MAIN = OFF_GATE + N_BRANCH * D_MODEL

CHUNK = 256
GATE_ROWS = 16
F_ROW = 8
CONV_PAD = 8
CONV_BLOCK = 64

LANES = 128
VMEM_LIMIT = 56 * 1024 * 1024


def _params(sem):
    return pltpu.CompilerParams(dimension_semantics=sem, vmem_limit_bytes=VMEM_LIMIT)


def _sigmoid(x):
    return 0.5 * jnp.tanh(0.5 * x) + 0.5


def _rms(xf, g):
    return xf * lax.rsqrt(jnp.mean(xf * xf, axis=-1, keepdims=True) + EPS) * g


def _head_norm(t, g):
    mu = jnp.mean(t, axis=-1, keepdims=True)
    c = t - mu
    var = jnp.mean(c * c, axis=-1, keepdims=True)
    return c * lax.rsqrt(var + EPS) * g


def _dot(a, b):
    return jnp.dot(a, b, preferred_element_type=F32)


def _memkv_kernel(m_ref, g_ref, w_ref, o_ref):
    h = _rms(m_ref[...], g_ref[...])
    o_ref[...] = _dot(h.astype(BF16), w_ref[...]).astype(BF16)


def _memkv(mem2, g, w):
    rows = mem2.shape[0]
    tm = min(512, rows)
    return pl.pallas_call(
        _memkv_kernel,
        grid=(rows // tm,),
        in_specs=[pl.BlockSpec((tm, D_MODEL), lambda i: (i, 0)),
                  pl.BlockSpec((1, D_MODEL), lambda i: (0, 0)),
                  pl.BlockSpec((D_MODEL, 2 * XA_W), lambda i: (0, 0))],
        out_specs=pl.BlockSpec((tm, 2 * XA_W), lambda i: (i, 0)),
        out_shape=jax.ShapeDtypeStruct((rows, 2 * XA_W), BF16),
        compiler_params=_params(("arbitrary",)),
        name="memkv",
    )(mem2, g, w)


def _inproj_kernel(x_ref, g_ref, w_ref, b_ref, wif_ref, bif_ref, p_ref, gate_ref, hb_ref, *, nsplit):
    j = pl.program_id(1)
    tx = x_ref.shape[0]

    @pl.when(j < nsplit)
    def _():
        hb = _rms(x_ref[...], g_ref[...]).astype(BF16)
        r0 = pl.multiple_of(j * tx, tx)
        hb_ref[pl.ds(r0, tx), :] = hb
        gt = lax.dot_general(wif_ref[...], hb, (((1,), (1,)), ((), ())), preferred_element_type=F32)
        gate_ref[:, pl.ds(r0, tx)] = gt + bif_ref[...]

    @pl.when(j >= nsplit)
    def _():
        acc = lax.dot_general(hb_ref[...], w_ref[...], (((1,), (1,)), ((), ())), preferred_element_type=F32)
        p_ref[...] = (acc + b_ref[...]).astype(BF16)


def _inproj(x2, g, w_main, b_main, wif_t, bif_t, tm, tn, nsplit):
    t = x2.shape[0]
    tx = tm // nsplit

    def col(j):
        return jnp.maximum(j - nsplit, 0)

    return pl.pallas_call(
        functools.partial(_inproj_kernel, nsplit=nsplit),
        grid=(t // tm, nsplit + N_MAIN // tn),
        in_specs=[pl.BlockSpec((tx, D_MODEL), lambda i, j: (i * nsplit + jnp.minimum(j, nsplit - 1), 0)),
                  pl.BlockSpec((1, D_MODEL), lambda i, j: (0, 0)),
                  pl.BlockSpec((tn, D_MODEL), lambda i, j: (col(j), 0)),
                  pl.BlockSpec((1, tn), lambda i, j: (0, col(j))),
                  pl.BlockSpec((GATE_ROWS, D_MODEL), lambda i, j: (0, 0)),
                  pl.BlockSpec((GATE_ROWS, 1), lambda i, j: (0, 0))],
        out_specs=[pl.BlockSpec((tm, tn), lambda i, j: (i, col(j))),
                   pl.BlockSpec((GATE_ROWS, tm), lambda i, j: (0, i))],
        out_shape=[jax.ShapeDtypeStruct((t, N_MAIN), BF16),
                   jax.ShapeDtypeStruct((GATE_ROWS, t), F32)],
        scratch_shapes=[pltpu.VMEM((tm, D_MODEL), BF16)],
        compiler_params=_params(("arbitrary", "arbitrary")),
        name="inproj",
    )(x2, g, w_main, b_main, wif_t, bif_t)


def _rope_kernel(pos_ref, freq_ref, sign_ref, cos_ref, sin_ref):
    ang = pos_ref[...].astype(F32) * freq_ref[...]
    cos_ref[...] = jnp.cos(ang)
    sin_ref[...] = jnp.sin(ang) * sign_ref[...]


def _rope_tables(pos_col, freq2, sign2):
    t = pos_col.shape[0]
    tm = min(2048, t)
    return pl.pallas_call(
        _rope_kernel,
        grid=(t // tm,),
        in_specs=[pl.BlockSpec((tm, 1), lambda i: (i, 0)),
                  pl.BlockSpec((1, RET_DQK), lambda i: (0, 0)),
                  pl.BlockSpec((1, RET_DQK), lambda i: (0, 0))],
        out_specs=[pl.BlockSpec((tm, RET_DQK), lambda i: (i, 0)),
                   pl.BlockSpec((tm, RET_DQK), lambda i: (i, 0))],
        out_shape=[jax.ShapeDtypeStruct((t, RET_DQK), F32),
                   jax.ShapeDtypeStruct((t, RET_DQK), F32)],
        compiler_params=_params(("arbitrary",)),
        name="rope",
    )(pos_col, freq2, sign2)


def _split3(x):
    hi = x.astype(BF16)
    r1 = x - hi.astype(F32)
    mid = r1.astype(BF16)
    lo = (r1 - mid.astype(F32)).astype(BF16)
    return hi, mid, lo


def _mlstm_kernel(q_ref, k_ref, v_ref, o_ref, z_ref, gi_ref, gf_ref, cwq_ref, cwk_ref, cbq_ref, cbk_ref,
                  hg_ref, out_ref, pad_ref, qc_ref, kc_ref, b_ref, a_ref, w_ref, mp_ref, mn_ref, dec_ref,
                  caug_ref, *, seq):
    L = CHUNK
    nchunk = seq // L

    pad_ref[pl.ds(0, CONV_PAD), :] = jnp.zeros((CONV_PAD, 2 * ML_DQK), F32)
    pad_ref[pl.ds(CONV_PAD, seq), pl.ds(0, ML_DQK)] = q_ref[...].astype(F32)
    pad_ref[pl.ds(CONV_PAD, seq), pl.ds(ML_DQK, ML_DQK)] = k_ref[...].astype(F32)
    cw = jnp.concatenate([cwq_ref[...], cwk_ref[...]], axis=-1)
    cb = jnp.concatenate([cbq_ref[...], cbk_ref[...]], axis=-1)
    first = CONV_PAD - (ML_CONV - 1)

    def conv_block(i, carry):
        r = pl.multiple_of(i * CONV_BLOCK, CONV_BLOCK)
        xx = pad_ref[pl.ds(r, CONV_BLOCK + CONV_PAD), :]
        y = cb
        for j in range(ML_CONV):
            y = y + xx[first + j:first + j + CONV_BLOCK, :] * cw[j:j + 1, :]
        y = y * _sigmoid(y)
        qc_ref[pl.ds(r, CONV_BLOCK), :] = y[:, :ML_DQK] * (ML_DQK ** -0.5)
        kc_ref[pl.ds(r, CONV_BLOCK), :] = y[:, ML_DQK:]
        return carry

    lax.fori_loop(0, seq // CONV_BLOCK, conv_block, 0)

    f = gf_ref[0]
    lf = jnp.minimum(f, 0.0) - jnp.log1p(jnp.exp(-jnp.abs(f)))
    row = lax.broadcasted_iota(jnp.int32, (L, L), 0)
    col = lax.broadcasted_iota(jnp.int32, (L, L), 1)
    triu = (row <= col).astype(BF16)
    hi, mid, lo = _split3(lf)
    bb = _dot(hi, triu) + _dot(mid, triu) + _dot(lo, triu)
    causal = row >= col
    diag = row == col

    gi = gi_ref[0]
    g_col = bb[:, L - 1:L]
    log_w = g_col - bb + gi
    w_max = jnp.max(log_w, axis=-1, keepdims=True)
    m = jnp.zeros((1, 1), F32)
    for c in range(nchunk):
        mp_ref[c:c + 1, :] = m
        m = jnp.maximum(g_col[c:c + 1, :] + m, w_max[c:c + 1, :])
        mn_ref[c:c + 1, :] = m
    m_prev_all = mp_ref[...]
    m_new_all = mn_ref[...]
    b_ref[...] = bb
    a_ref[...] = gi - bb
    w_ref[...] = jnp.exp(log_w - m_new_all)
    dec_ref[...] = jnp.exp(g_col + m_prev_all - m_new_all)

    caug_ref[...] = jnp.zeros_like(caug_ref)
    hg_half = 0.5 * hg_ref[...]
    ones_blk = jnp.ones((L, LANES), BF16)

    def chunk(c, carry):
        r0 = pl.multiple_of(c * L, L)
        rows = pl.ds(r0, L)
        one = pl.ds(c, 1)
        b_row = b_ref[one, :]
        a_row = a_ref[one, :]
        w_row = w_ref[one, :]
        m_prev = mp_ref[one, :]
        decay = dec_ref[one, :]
        a_mat = jnp.where(causal, a_row, -jnp.inf)
        u = jnp.maximum(m_prev, jnp.max(a_mat, axis=-1, keepdims=True))
        d = jnp.exp(a_mat - u)
        b_col = jnp.sum(jnp.where(diag, b_row, 0.0), axis=-1, keepdims=True)
        inter = jnp.exp(m_prev - u)
        qf = qc_ref[rows, :]
        kf = kc_ref[rows, :]
        qb = qf.astype(BF16)
        kt = kf.T
        vb = v_ref[rows, :]
        s = _dot(qb, kt.astype(BF16)) * d
        caug = caug_ref[...]
        vaug = jnp.concatenate([vb, ones_blk], axis=1)
        na = _dot(s.astype(BF16), vaug) + inter * _dot(qb, caug.astype(BF16))
        den = jnp.maximum(jnp.abs(na[:, ML_DV:]), jnp.exp(-(b_col + u)))
        hh = na[:, :ML_DV] * jnp.tile(1.0 / den, (1, ML_DV // LANES))

        caug_ref[...] = decay * caug + _dot((kt * w_row).astype(BF16), vaug)

        y = _head_norm(hh, hg_half)
        oh = 0.5 * o_ref[rows, :].astype(F32)
        zh = 0.5 * z_ref[rows, :].astype(F32)
        y = y * (1.0 + jnp.tanh(oh)) * (zh * (1.0 + jnp.tanh(zh)))
        out_ref[rows, :] = y.astype(BF16)
        return carry

    lax.fori_loop(0, nchunk, chunk, 0, unroll=4)


def _mlstm(p, gates3, conv_w, conv_b, hnorm_g, batch, seq):
    t = batch * seq
    nchunk = seq // CHUNK
    qb, vb = ML_DQK, ML_DV

    def pspec(width, off):
        return pl.BlockSpec((seq, width), lambda b, h, off=off // width: (b, off + h))

    def gspec(row0):
        return pl.BlockSpec((1, nchunk, CHUNK), lambda b, h, row0=row0: (row0 + h, b, 0))

    kern = functools.partial(_mlstm_kernel, seq=seq)
    return pl.pallas_call(
        kern,
        grid=(batch, ML_HEADS),
        in_specs=[pspec(qb, OFF_ML_Q), pspec(qb, OFF_ML_K), pspec(vb, OFF_ML_V), pspec(vb, OFF_ML_O),
                  pspec(vb, OFF_ML_Z), gspec(0), gspec(F_ROW),
                  pl.BlockSpec((ML_CONV, qb), lambda b, h: (0, h)),
                  pl.BlockSpec((ML_CONV, qb), lambda b, h: (0, ML_HEADS + h)),
                  pl.BlockSpec((1, qb), lambda b, h: (0, h)),
                  pl.BlockSpec((1, qb), lambda b, h: (0, ML_HEADS + h)),
                  pl.BlockSpec((1, vb), lambda b, h: (0, h))],
        out_specs=pl.BlockSpec((seq, vb), lambda b, h: (b, h)),
        out_shape=jax.ShapeDtypeStruct((t, ML_V), BF16),
        scratch_shapes=[pltpu.VMEM((seq + CONV_PAD, 2 * qb), F32),
                        pltpu.VMEM((seq, qb), F32),
                        pltpu.VMEM((seq, qb), F32),
                        pltpu.VMEM((nchunk, CHUNK), F32),
                        pltpu.VMEM((nchunk, CHUNK), F32),
                        pltpu.VMEM((nchunk, CHUNK), F32),
                        pltpu.VMEM((nchunk, 1), F32),
                        pltpu.VMEM((nchunk, 1), F32),
                        pltpu.VMEM((nchunk, 1), F32),
                        pltpu.VMEM((qb, vb + LANES), F32)],
        compiler_params=_params(("arbitrary", "arbitrary")),
        name="mlstm",
    )(p, p, p, p, p, gates3, gates3, conv_w, conv_w, conv_b, conv_b, hnorm_g)


def _ret_kernel(lg_ref, q_ref, k_ref, v_ref, z_ref, cos_ref, sin_ref, hg_ref, out_ref, r_ref, *, seq):
    L = CHUNK
    nchunk = seq // L
    lg = lg_ref[pl.program_id(1)]
    row = lax.broadcasted_iota(jnp.int32, (L, L), 0)
    col = lax.broadcasted_iota(jnp.int32, (L, L), 1)
    intra = jnp.where(row >= col, jnp.exp((row - col).astype(F32) * lg), 0.0)
    pos_col = lax.broadcasted_iota(jnp.int32, (L, 1), 0).astype(F32)
    pos_row = lax.broadcasted_iota(jnp.int32, (1, L), 1).astype(F32)
    q_decay = jnp.exp((pos_col + 1.0) * lg)
    k_decay = jnp.exp((L - 1.0 - pos_row) * lg)
    chunk_decay = jnp.exp(jnp.full((1, 1), float(L), F32) * lg)
    r_ref[...] = jnp.zeros_like(r_ref)
    half = RET_DQK // 2
    hg = hg_ref[...]

    def chunk(c, carry):
        r0 = pl.multiple_of(c * L, L)
        rows = pl.ds(r0, L)
        cs = cos_ref[rows, :]
        sn = sin_ref[rows, :]
        qf = q_ref[rows, :].astype(F32)
        kf = k_ref[rows, :].astype(F32)
        qr = qf * cs + pltpu.roll(qf, half, 1) * sn
        kr = (kf * cs + pltpu.roll(kf, half, 1) * sn) * (RET_DQK ** -0.5)
        qb = qr.astype(BF16)
        kt = kr.T
        vb = v_ref[rows, :]
        s = _dot(qb, kt.astype(BF16)) * intra
        r_state = r_ref[...]
        o = _dot(s.astype(BF16), vb) + q_decay * _dot(qb, r_state.astype(BF16))
        r_ref[...] = chunk_decay * r_state + _dot((kt * k_decay).astype(BF16), vb)
        y = _head_norm(o, hg)
        zh = 0.5 * z_ref[rows, :].astype(F32)
        out_ref[rows, :] = (y * (zh * (1.0 + jnp.tanh(zh)))).astype(BF16)
        return carry

    lax.fori_loop(0, nchunk, chunk, 0, unroll=4)


def _retention(log_gamma, p, cos2, sin2, hnorm_g, batch, seq):
    t = batch * seq
    qb, vb = RET_DQK, RET_DV

    def pspec(width, off):
        return pl.BlockSpec((seq, width), lambda b, h, lg, off=off // width: (b, off + h))

    kern = functools.partial(_ret_kernel, seq=seq)
    grid_spec = pltpu.PrefetchScalarGridSpec(
        num_scalar_prefetch=1,
        grid=(batch, RET_HEADS),
        in_specs=[pspec(qb, OFF_RT_Q), pspec(qb, OFF_RT_K), pspec(vb, OFF_RT_V), pspec(vb, OFF_RT_Z),
                  pl.BlockSpec((seq, qb), lambda b, h, lg: (b, 0)),
                  pl.BlockSpec((seq, qb), lambda b, h, lg: (b, 0)),
                  pl.BlockSpec((1, vb), lambda b, h, lg: (0, h))],
        out_specs=pl.BlockSpec((seq, vb), lambda b, h, lg: (b, h)),
        scratch_shapes=[pltpu.VMEM((qb, vb), F32)],
    )
    return pl.pallas_call(
        kern,
        grid_spec=grid_spec,
        out_shape=jax.ShapeDtypeStruct((t, RET_V), BF16),
        compiler_params=_params(("arbitrary", "arbitrary")),
        name="retention",
    )(log_gamma, p, p, p, p, cos2, sin2, hnorm_g)


def _xattn_kernel(q_ref, z_ref, mk_ref, mv_ref, out_ref):
    sc = lax.dot_general(q_ref[...], mk_ref[...], (((1,), (1,)), ((), ())), preferred_element_type=F32)
    sc = sc * (XA_DH ** -0.5)
    e = jnp.exp(sc - jnp.max(sc, axis=-1, keepdims=True))
    o = _dot(e.astype(BF16), mv_ref[...]) * (1.0 / jnp.sum(e, axis=-1, keepdims=True))
    zf = z_ref[...].astype(F32)
    out_ref[...] = (o * (zf * _sigmoid(zf))).astype(BF16)


def _xattn(p, memkv, batch, seq):
    t = batch * seq
    ts = min(1024, seq)
    ns = seq // ts
    w = XA_DH
    return pl.pallas_call(
        _xattn_kernel,
        grid=(batch, XA_HEADS, ns),
        in_specs=[pl.BlockSpec((ts, w), lambda b, h, s: (b * ns + s, OFF_XA_Q // w + h)),
                  pl.BlockSpec((ts, w), lambda b, h, s: (b * ns + s, OFF_XA_Z // w + h)),
                  pl.BlockSpec((MEM_TOKENS, w), lambda b, h, s: (b, h)),
                  pl.BlockSpec((MEM_TOKENS, w), lambda b, h, s: (b, XA_HEADS + h))],
        out_specs=pl.BlockSpec((ts, w), lambda b, h, s: (b * ns + s, h)),
        out_shape=jax.ShapeDtypeStruct((t, XA_W), BF16),
        compiler_params=_params(("arbitrary", "arbitrary", "arbitrary")),
        name="xattn",
    )(p, p, memkv, memkv)


def _merge_kernel(ml_ref, rt_ref, xa_ref, wml_ref, wrt_ref, wxa_ref, g0_ref, g1_ref, g2_ref, o_ref):
    acc = _sigmoid(g0_ref[...].astype(F32)) * _dot(ml_ref[...], wml_ref[...])
    acc = acc + _sigmoid(g1_ref[...].astype(F32)) * _dot(rt_ref[...], wrt_ref[...])
    acc = acc + _sigmoid(g2_ref[...].astype(F32)) * _dot(xa_ref[...], wxa_ref[...])
    o_ref[...] = acc.astype(BF16)


def _merge(ml_out, rt_out, xa_out, w_ml, w_rt, w_xa, p, tm, tn):
    t = ml_out.shape[0]
    nj = D_MODEL // tn

    def gspec(br):
        return pl.BlockSpec((tm, tn), lambda i, j, o=(OFF_GATE + br * D_MODEL) // tn: (i, o + j))

    return pl.pallas_call(
        _merge_kernel,
        grid=(t // tm, nj),
        in_specs=[pl.BlockSpec((tm, ML_V), lambda i, j: (i, 0)),
                  pl.BlockSpec((tm, RET_V), lambda i, j: (i, 0)),
                  pl.BlockSpec((tm, XA_W), lambda i, j: (i, 0)),
                  pl.BlockSpec((ML_V, tn), lambda i, j: (0, j)),
                  pl.BlockSpec((RET_V, tn), lambda i, j: (0, j)),
                  pl.BlockSpec((XA_W, tn), lambda i, j: (0, j)),
                  gspec(0), gspec(1), gspec(2)],
        out_specs=pl.BlockSpec((tm, tn), lambda i, j: (i, j)),
        out_shape=jax.ShapeDtypeStruct((t, D_MODEL), BF16),
        compiler_params=_params(("arbitrary", "arbitrary")),
        name="merge",
    )(ml_out, rt_out, xa_out, w_ml, w_rt, w_xa, p, p, p)


def _outproj_kernel(x_ref, m_ref, w_ref, g_ref, o_ref):
    y = x_ref[...] + _dot(m_ref[...], w_ref[...])
    o_ref[...] = _rms(y, g_ref[...])


def _outproj(x2, merged, w_out, final_g, tm):
    t = x2.shape[0]
    return pl.pallas_call(
        _outproj_kernel,
        grid=(t // tm,),
        in_specs=[pl.BlockSpec((tm, D_MODEL), lambda i: (i, 0)),
                  pl.BlockSpec((tm, D_MODEL), lambda i: (i, 0)),
                  pl.BlockSpec((D_MODEL, D_MODEL), lambda i: (0, 0)),
                  pl.BlockSpec((1, D_MODEL), lambda i: (0, 0))],
        out_specs=pl.BlockSpec((tm, D_MODEL), lambda i: (i, 0)),
        out_shape=jax.ShapeDtypeStruct((t, D_MODEL), F32),
        compiler_params=_params(("arbitrary",)),
        name="outproj",
    )(x2, merged, w_out, final_g)


def kernel(x, mem, positions, ln_g, mem_ln_g, w_in, b_in, conv_w, conv_b, ml_hnorm_g, ret_hnorm_g, w_mem_kv,
           w_br_ml, w_br_ret, w_br_xa, w_out, final_g):
    batch, seq, _ = x.shape
    assert seq % CHUNK == 0 and ln_g.shape[0] == 1
    t = batch * seq
    x2 = x.reshape(t, D_MODEL)

    wt, b0 = w_in[0].T, b_in[0]
    w_main = jnp.concatenate([wt[:OFF_IF], wt[OFF_IF + 2 * ML_HEADS:]], axis=0).astype(BF16)
    b_main = jnp.concatenate([b0[:OFF_IF], b0[OFF_IF + 2 * ML_HEADS:]])[None, :]
    wif_t = jnp.zeros((GATE_ROWS, D_MODEL), F32)
    wif_t = wif_t.at[0:ML_HEADS].set(wt[OFF_IF:OFF_IF + ML_HEADS])
    wif_t = wif_t.at[F_ROW:F_ROW + ML_HEADS].set(wt[OFF_IF + ML_HEADS:OFF_IF + 2 * ML_HEADS]).astype(BF16)
    bif_t = jnp.zeros((GATE_ROWS, 1), F32)
    bif_t = bif_t.at[0:ML_HEADS, 0].set(b0[OFF_IF:OFF_IF + ML_HEADS])
    bif_t = bif_t.at[F_ROW:F_ROW + ML_HEADS, 0].set(b0[OFF_IF + ML_HEADS:OFF_IF + 2 * ML_HEADS])

    memkv = _memkv(mem.reshape(batch * MEM_TOKENS, D_MODEL), mem_ln_g[0][None, :], w_mem_kv[0].astype(BF16))

    tm = min(1024, t)
    p, gates_t = _inproj(x2, ln_g[0][None, :], w_main, b_main, wif_t, bif_t, min(2048, t), 512, 2)
    gates3 = gates_t.reshape(GATE_ROWS, t // CHUNK, CHUNK)

    half = RET_DQK // 2
    freqs = ROPE_BASE ** (-jnp.arange(half, dtype=F32) / half)
    freq2 = jnp.concatenate([freqs, freqs])[None, :]
    sign2 = jnp.concatenate([-jnp.ones((half,), F32), jnp.ones((half,), F32)])[None, :]
    cos2, sin2 = _rope_tables(positions.reshape(t, 1), freq2, sign2)
    log_gamma = jnp.asarray(np.log(1.0 - 2.0 ** (-5.0 - np.arange(RET_HEADS))), dtype=F32)

    ml_out = _mlstm(p, gates3, conv_w[0], conv_b[0][None, :], ml_hnorm_g[0][None, :], batch, seq)
    rt_out = _retention(log_gamma, p, cos2, sin2, ret_hnorm_g[0][None, :], batch, seq)
    xa_out = _xattn(p, memkv, batch, seq)

    merged = _merge(ml_out, rt_out, xa_out, w_br_ml[0].astype(BF16), w_br_ret[0].astype(BF16),
                    w_br_xa[0].astype(BF16), p, tm, 512)
    out = _outproj(x2, merged, w_out[0].astype(BF16), final_g[None, :], min(512, t))
    return out.reshape(batch, seq, D_MODEL)
```

```python
import functools

import numpy as np
import jax
import jax.numpy as jnp
from jax import lax
from jax.experimental import pallas as pl
from jax.experimental.pallas import tpu as pltpu

F32 = jnp.float32
BF16 = jnp.bfloat16

D_MODEL = 2048
MEM_TOKENS = 256
ML_HEADS = 6
ML_DQK = 128
ML_DV = 256
ML_CONV = 4
RET_HEADS = 6
RET_DQK = 128
RET_DV = 256
XA_HEADS = 4
XA_DH = 256
ROPE_BASE = 10000.0
EPS = 1e-6
N_BRANCH = 3

ML_QK = ML_HEADS * ML_DQK
ML_V = ML_HEADS * ML_DV
RET_QK = RET_HEADS * RET_DQK
RET_V = RET_HEADS * RET_DV
XA_W = XA_HEADS * XA_DH

OFF_ML_Q = 0
OFF_ML_K = OFF_ML_Q + ML_QK
OFF_ML_V = OFF_ML_K + ML_QK
OFF_ML_O = OFF_ML_V + ML_V
OFF_ML_Z = OFF_ML_O + ML_V
OFF_IF = OFF_ML_Z + ML_V
OFF_RT_Q = OFF_ML_Z + ML_V
OFF_RT_K = OFF_RT_Q + RET_QK
OFF_RT_V = OFF_RT_K + RET_QK
OFF_RT_Z = OFF_RT_V + RET_V
OFF_XA_Q = OFF_RT_Z + RET_V
OFF_XA_Z = OFF_XA_Q + XA_W
OFF_GATE = OFF_XA_Z + XA_W
N_MAIN = OFF_GATE + N_BRANCH * D_MODEL

CHUNK = 256
GATE_ROWS = 16
F_ROW = 8
CONV_PAD = 8
CONV_BLOCK = 64

LANES = 128
VMEM_LIMIT = 56 * 1024 * 1024


def _params(sem):
    return pltpu.CompilerParams(dimension_semantics=sem, vmem_limit_bytes=VMEM_LIMIT)


def _sigmoid(x):
    return 0.5 * jnp.tanh(0.5 * x) + 0.5


def _rms(xf, g):
    return xf * lax.rsqrt(jnp.mean(xf * xf, axis=-1, keepdims=True) + EPS) * g


def _head_norm(t, g):
    mu = jnp.mean(t, axis=-1, keepdims=True)
    c = t - mu
    var = jnp.mean(c * c, axis=-1, keepdims=True)
    return c * lax.rsqrt(var + EPS) * g


def _dot(a, b):
    return jnp.dot(a, b, preferred_element_type=F32)


def _memkv_kernel(m_ref, g_ref, w_ref, o_ref):
    h = _rms(m_ref[...], g_ref[...])
    o_ref[...] = _dot(h.astype(BF16), w_ref[...]).astype(BF16)


def _memkv(mem2, g, w):
    rows = mem2.shape[0]
    tm = min(512, rows)
    return pl.pallas_call(
        _memkv_kernel,
        grid=(rows // tm,),
        in_specs=[pl.BlockSpec((tm, D_MODEL), lambda i: (i, 0)),
                  pl.BlockSpec((1, D_MODEL), lambda i: (0, 0)),
                  pl.BlockSpec((D_MODEL, 2 * XA_W), lambda i: (0, 0))],
        out_specs=pl.BlockSpec((tm, 2 * XA_W), lambda i: (i, 0)),
        out_shape=jax.ShapeDtypeStruct((rows, 2 * XA_W), BF16),
        compiler_params=_params(("arbitrary",)),
        name="memkv",
    )(mem2, g, w)


def _inproj_kernel(x_ref, g_ref, w_ref, b_ref, wif_ref, bif_ref, p_ref, gate_ref, hb_ref, *, nsplit, tail):
    j = pl.program_id(1)
    last = pl.num_programs(1) - 1
    tx = x_ref.shape[0]
    tn = w_ref.shape[0]

    def project(width):
        acc = lax.dot_general(hb_ref[...], w_ref[pl.ds(0, width), :], (((1,), (1,)), ((), ())),
                              preferred_element_type=F32)
        p_ref[:, pl.ds(0, width)] = (acc + b_ref[:, pl.ds(0, width)]).astype(BF16)

    @pl.when(j < nsplit)
    def _():
        hb = _rms(x_ref[...], g_ref[...]).astype(BF16)
        r0 = pl.multiple_of(j * tx, tx)
        hb_ref[pl.ds(r0, tx), :] = hb
        gt = lax.dot_general(wif_ref[...], hb, (((1,), (1,)), ((), ())), preferred_element_type=F32)
        gate_ref[:, pl.ds(r0, tx)] = gt + bif_ref[...]

    if tail == tn:
        pl.when(j >= nsplit)(lambda: project(tn))
    else:
        pl.when(jnp.logical_and(j >= nsplit, j < last))(lambda: project(tn))
        pl.when(j == last)(lambda: project(tail))


def _inproj(x2, g, w_main, b_main, wif_t, bif_t, tm, tn, nsplit):
    t = x2.shape[0]
    tx = tm // nsplit
    ncol = pl.cdiv(N_MAIN, tn)
    tail = N_MAIN - (ncol - 1) * tn

    def col(j):
        return jnp.maximum(j - nsplit, 0)

    return pl.pallas_call(
        functools.partial(_inproj_kernel, nsplit=nsplit, tail=tail),
        grid=(t // tm, nsplit + ncol),
        in_specs=[pl.BlockSpec((tx, D_MODEL), lambda i, j: (i * nsplit + jnp.minimum(j, nsplit - 1), 0)),
                  pl.BlockSpec((1, D_MODEL), lambda i, j: (0, 0)),
                  pl.BlockSpec((tn, D_MODEL), lambda i, j: (col(j), 0)),
                  pl.BlockSpec((1, tn), lambda i, j: (0, col(j))),
                  pl.BlockSpec((GATE_ROWS, D_MODEL), lambda i, j: (0, 0)),
                  pl.BlockSpec((GATE_ROWS, 1), lambda i, j: (0, 0))],
        out_specs=[pl.BlockSpec((tm, tn), lambda i, j: (i, col(j))),
                   pl.BlockSpec((GATE_ROWS, tm), lambda i, j: (0, i))],
        out_shape=[jax.ShapeDtypeStruct((t, N_MAIN), BF16),
                   jax.ShapeDtypeStruct((GATE_ROWS, t), F32)],
        scratch_shapes=[pltpu.VMEM((tm, D_MODEL), BF16)],
        compiler_params=_params(("arbitrary", "arbitrary")),
        name="inproj",
    )(x2, g, w_main, b_main, wif_t, bif_t)


def _rope_kernel(pos_ref, freq_ref, sign_ref, cos_ref, sin_ref):
    ang = pos_ref[...].astype(F32) * freq_ref[...]
    cos_ref[...] = jnp.cos(ang)
    sin_ref[...] = jnp.sin(ang) * sign_ref[...]


def _rope_tables(pos_col, freq2, sign2):
    t = pos_col.shape[0]
    tm = min(2048, t)
    return pl.pallas_call(
        _rope_kernel,
        grid=(t // tm,),
        in_specs=[pl.BlockSpec((tm, 1), lambda i: (i, 0)),
                  pl.BlockSpec((1, RET_DQK), lambda i: (0, 0)),
                  pl.BlockSpec((1, RET_DQK), lambda i: (0, 0))],
        out_specs=[pl.BlockSpec((tm, RET_DQK), lambda i: (i, 0)),
                   pl.BlockSpec((tm, RET_DQK), lambda i: (i, 0))],
        out_shape=[jax.ShapeDtypeStruct((t, RET_DQK), F32),
                   jax.ShapeDtypeStruct((t, RET_DQK), F32)],
        compiler_params=_params(("arbitrary",)),
        name="rope",
    )(pos_col, freq2, sign2)


def _split3(x):
    hi = x.astype(BF16)
    r1 = x - hi.astype(F32)
    mid = r1.astype(BF16)
    lo = (r1 - mid.astype(F32)).astype(BF16)
    return hi, mid, lo


def _mlstm_kernel(q_ref, k_ref, v_ref, o_ref, z_ref, gi_ref, gf_ref, cwq_ref, cwk_ref, cbq_ref, cbk_ref,
                  hg_ref, out_ref, pad_ref, qc_ref, kc_ref, b_ref, a_ref, w_ref, mp_ref, mn_ref, dec_ref,
                  caug_ref, *, seq):
    L = CHUNK
    nchunk = seq // L

    pad_ref[pl.ds(0, CONV_PAD), :] = jnp.zeros((CONV_PAD, 2 * ML_DQK), F32)
    pad_ref[pl.ds(CONV_PAD, seq), pl.ds(0, ML_DQK)] = q_ref[...].astype(F32)
    pad_ref[pl.ds(CONV_PAD, seq), pl.ds(ML_DQK, ML_DQK)] = k_ref[...].astype(F32)
    cw = jnp.concatenate([cwq_ref[...], cwk_ref[...]], axis=-1)
    cb = jnp.concatenate([cbq_ref[...], cbk_ref[...]], axis=-1)
    first = CONV_PAD - (ML_CONV - 1)

    def conv_block(i, carry):
        r = pl.multiple_of(i * CONV_BLOCK, CONV_BLOCK)
        xx = pad_ref[pl.ds(r, CONV_BLOCK + CONV_PAD), :]
        y = cb
        for j in range(ML_CONV):
            y = y + xx[first + j:first + j + CONV_BLOCK, :] * cw[j:j + 1, :]
        y = y * _sigmoid(y)
        qc_ref[pl.ds(r, CONV_BLOCK), :] = y[:, :ML_DQK] * (ML_DQK ** -0.5)
        kc_ref[pl.ds(r, CONV_BLOCK), :] = y[:, ML_DQK:]
        return carry

    lax.fori_loop(0, seq // CONV_BLOCK, conv_block, 0)

    f = gf_ref[0]
    lf = jnp.minimum(f, 0.0) - jnp.log1p(jnp.exp(-jnp.abs(f)))
    row = lax.broadcasted_iota(jnp.int32, (L, L), 0)
    col = lax.broadcasted_iota(jnp.int32, (L, L), 1)
    triu = (row <= col).astype(BF16)
    hi, mid, lo = _split3(lf)
    bb = _dot(hi, triu) + _dot(mid, triu) + _dot(lo, triu)
    causal = row >= col
    diag = row == col

    gi = gi_ref[0]
    g_col = bb[:, L - 1:L]
    log_w = g_col - bb + gi
    w_max = jnp.max(log_w, axis=-1, keepdims=True)
    m = jnp.zeros((1, 1), F32)
    for c in range(nchunk):
        mp_ref[c:c + 1, :] = m
        m = jnp.maximum(g_col[c:c + 1, :] + m, w_max[c:c + 1, :])
        mn_ref[c:c + 1, :] = m
    m_prev_all = mp_ref[...]
    m_new_all = mn_ref[...]
    b_ref[...] = bb
    a_ref[...] = gi - bb
    w_ref[...] = jnp.exp(log_w - m_new_all)
    dec_ref[...] = jnp.exp(g_col + m_prev_all - m_new_all)

    caug_ref[...] = jnp.zeros_like(caug_ref)
    hg_half = 0.5 * hg_ref[...]
    ones_blk = jnp.ones((L, LANES), BF16)

    def chunk(c, carry):
        r0 = pl.multiple_of(c * L, L)
        rows = pl.ds(r0, L)
        one = pl.ds(c, 1)
        b_row = b_ref[one, :]
        a_row = a_ref[one, :]
        w_row = w_ref[one, :]
        m_prev = mp_ref[one, :]
        decay = dec_ref[one, :]
        a_mat = jnp.where(causal, a_row, -jnp.inf)
        u = jnp.maximum(m_prev, jnp.max(a_mat, axis=-1, keepdims=True))
        d = jnp.exp(a_mat - u)
        b_col = jnp.sum(jnp.where(diag, b_row, 0.0), axis=-1, keepdims=True)
        inter = jnp.exp(m_prev - u)
        qf = qc_ref[rows, :]
        kf = kc_ref[rows, :]
        qb = qf.astype(BF16)
        kt = kf.T
        vb = v_ref[rows, :]
        s = _dot(qb, kt.astype(BF16)) * d
        caug = caug_ref[...]
        vaug = jnp.concatenate([vb, ones_blk], axis=1)
        na = _dot(s.astype(BF16), vaug) + inter * _dot(qb, caug.astype(BF16))
        den = jnp.maximum(jnp.abs(na[:, ML_DV:]), jnp.exp(-(b_col + u)))
        hh = na[:, :ML_DV] * jnp.tile(1.0 / den, (1, ML_DV // LANES))

        caug_ref[...] = decay * caug + _dot((kt * w_row).astype(BF16), vaug)

        y = _head_norm(hh, hg_half)
        oh = 0.5 * o_ref[rows, :].astype(F32)
        zh = 0.5 * z_ref[rows, :].astype(F32)
        y = y * (1.0 + jnp.tanh(oh)) * (zh * (1.0 + jnp.tanh(zh)))
        out_ref[rows, :] = y.astype(BF16)
        return carry

    lax.fori_loop(0, nchunk, chunk, 0, unroll=4)


def _mlstm(p, gates3, conv_w, conv_b, hnorm_g, batch, seq):
    t = batch * seq
    nchunk = seq // CHUNK
    qb, vb = ML_DQK, ML_DV

    def pspec(width, off):
        return pl.BlockSpec((seq, width), lambda b, h, off=off // width: (b, off + h))

    def gspec(row0):
        return pl.BlockSpec((1, nchunk, CHUNK), lambda b, h, row0=row0: (row0 + h, b, 0))

    kern = functools.partial(_mlstm_kernel, seq=seq)
    return pl.pallas_call(
        kern,
        grid=(batch, ML_HEADS),
        in_specs=[pspec(qb, OFF_ML_Q), pspec(qb, OFF_ML_K), pspec(vb, OFF_ML_V), pspec(vb, OFF_ML_O),
                  pspec(vb, OFF_ML_Z), gspec(0), gspec(F_ROW),
                  pl.BlockSpec((ML_CONV, qb), lambda b, h: (0, h)),
                  pl.BlockSpec((ML_CONV, qb), lambda b, h: (0, ML_HEADS + h)),
                  pl.BlockSpec((1, qb), lambda b, h: (0, h)),
                  pl.BlockSpec((1, qb), lambda b, h: (0, ML_HEADS + h)),
                  pl.BlockSpec((1, vb), lambda b, h: (0, h))],
        out_specs=pl.BlockSpec((seq, vb), lambda b, h: (b, h)),
        out_shape=jax.ShapeDtypeStruct((t, ML_V), BF16),
        scratch_shapes=[pltpu.VMEM((seq + CONV_PAD, 2 * qb), F32),
                        pltpu.VMEM((seq, qb), F32),
                        pltpu.VMEM((seq, qb), F32),
                        pltpu.VMEM((nchunk, CHUNK), F32),
                        pltpu.VMEM((nchunk, CHUNK), F32),
                        pltpu.VMEM((nchunk, CHUNK), F32),
                        pltpu.VMEM((nchunk, 1), F32),
                        pltpu.VMEM((nchunk, 1), F32),
                        pltpu.VMEM((nchunk, 1), F32),
                        pltpu.VMEM((qb, vb + LANES), F32)],
        compiler_params=_params(("arbitrary", "arbitrary")),
        name="mlstm",
    )(p, p, p, p, p, gates3, gates3, conv_w, conv_w, conv_b, conv_b, hnorm_g)


def _ret_kernel(lg_ref, q_ref, k_ref, v_ref, z_ref, cos_ref, sin_ref, hg_ref, out_ref, r_ref, *, seq):
    L = CHUNK
    nchunk = seq // L
    lg = lg_ref[pl.program_id(1)]
    row = lax.broadcasted_iota(jnp.int32, (L, L), 0)
    col = lax.broadcasted_iota(jnp.int32, (L, L), 1)
    intra = jnp.where(row >= col, jnp.exp((row - col).astype(F32) * lg), 0.0)
    pos_col = lax.broadcasted_iota(jnp.int32, (L, 1), 0).astype(F32)
    pos_row = lax.broadcasted_iota(jnp.int32, (1, L), 1).astype(F32)
    q_decay = jnp.exp((pos_col + 1.0) * lg)
    k_decay = jnp.exp((L - 1.0 - pos_row) * lg)
    chunk_decay = jnp.exp(jnp.full((1, 1), float(L), F32) * lg)
    r_ref[...] = jnp.zeros_like(r_ref)
    half = RET_DQK // 2
    hg = hg_ref[...]

    def chunk(c, carry):
        r0 = pl.multiple_of(c * L, L)
        rows = pl.ds(r0, L)
        cs = cos_ref[rows, :]
        sn = sin_ref[rows, :]
        qf = q_ref[rows, :].astype(F32)
        kf = k_ref[rows, :].astype(F32)
        qr = qf * cs + pltpu.roll(qf, half, 1) * sn
        kr = (kf * cs + pltpu.roll(kf, half, 1) * sn) * (RET_DQK ** -0.5)
        qb = qr.astype(BF16)
        kt = kr.T
        vb = v_ref[rows, :]
        s = _dot(qb, kt.astype(BF16)) * intra
        r_state = r_ref[...]
        o = _dot(s.astype(BF16), vb) + q_decay * _dot(qb, r_state.astype(BF16))
        r_ref[...] = chunk_decay * r_state + _dot((kt * k_decay).astype(BF16), vb)
        y = _head_norm(o, hg)
        zh = 0.5 * z_ref[rows, :].astype(F32)
        out_ref[rows, :] = (y * (zh * (1.0 + jnp.tanh(zh)))).astype(BF16)
        return carry

    lax.fori_loop(0, nchunk, chunk, 0, unroll=4)


def _retention(log_gamma, p, cos2, sin2, hnorm_g, batch, seq):
    t = batch * seq
    qb, vb = RET_DQK, RET_DV

    def pspec(width, off):
        return pl.BlockSpec((seq, width), lambda b, h, lg, off=off // width: (b, off + h))

    kern = functools.partial(_ret_kernel, seq=seq)
    grid_spec = pltpu.PrefetchScalarGridSpec(
        num_scalar_prefetch=1,
        grid=(batch, RET_HEADS),
        in_specs=[pspec(qb, OFF_RT_Q), pspec(qb, OFF_RT_K), pspec(vb, OFF_RT_V), pspec(vb, OFF_RT_Z),
                  pl.BlockSpec((seq, qb), lambda b, h, lg: (b, 0)),
                  pl.BlockSpec((seq, qb), lambda b, h, lg: (b, 0)),
                  pl.BlockSpec((1, vb), lambda b, h, lg: (0, h))],
        out_specs=pl.BlockSpec((seq, vb), lambda b, h, lg: (b, h)),
        scratch_shapes=[pltpu.VMEM((qb, vb), F32)],
    )
    return pl.pallas_call(
        kern,
        grid_spec=grid_spec,
        out_shape=jax.ShapeDtypeStruct((t, RET_V), BF16),
        compiler_params=_params(("arbitrary", "arbitrary")),
        name="retention",
    )(log_gamma, p, p, p, p, cos2, sin2, hnorm_g)


def _xattn_kernel(q_ref, z_ref, mk_ref, mv_ref, out_ref, *, blk):
    mk = mk_ref[...]
    mv = mv_ref[...]

    def block(i, carry):
        rows = pl.ds(pl.multiple_of(i * blk, blk), blk)
        sc = lax.dot_general(q_ref[rows, :], mk, (((1,), (1,)), ((), ())), preferred_element_type=F32)
        sc = sc * (XA_DH ** -0.5)
        e = jnp.exp(sc - jnp.max(sc, axis=-1, keepdims=True))
        o = _dot(e.astype(BF16), mv) * (1.0 / jnp.sum(e, axis=-1, keepdims=True))
        zh = 0.5 * z_ref[rows, :].astype(F32)
        out_ref[rows, :] = (o * (zh * (1.0 + jnp.tanh(zh)))).astype(BF16)
        return carry

    lax.fori_loop(0, q_ref.shape[0] // blk, block, 0)


def _xattn(p, memkv, batch, seq):
    t = batch * seq
    ts = seq
    ns = seq // ts
    w = XA_DH
    return pl.pallas_call(
        functools.partial(_xattn_kernel, blk=min(1024, ts)),
        grid=(batch, XA_HEADS, ns),
        in_specs=[pl.BlockSpec((ts, w), lambda b, h, s: (b * ns + s, OFF_XA_Q // w + h)),
                  pl.BlockSpec((ts, w), lambda b, h, s: (b * ns + s, OFF_XA_Z // w + h)),
                  pl.BlockSpec((MEM_TOKENS, w), lambda b, h, s: (b, h)),
                  pl.BlockSpec((MEM_TOKENS, w), lambda b, h, s: (b, XA_HEADS + h))],
        out_specs=pl.BlockSpec((ts, w), lambda b, h, s: (b * ns + s, h)),
        out_shape=jax.ShapeDtypeStruct((t, XA_W), BF16),
        compiler_params=_params(("arbitrary", "arbitrary", "arbitrary")),
        name="xattn",
    )(p, p, memkv, memkv)


def _merge_kernel(ml_ref, rt_ref, xa_ref, wml_ref, wrt_ref, wxa_ref, g0_ref, g1_ref, g2_ref, o_ref):
    acc = _sigmoid(g0_ref[...].astype(F32)) * _dot(ml_ref[...], wml_ref[...])
    acc = acc + _sigmoid(g1_ref[...].astype(F32)) * _dot(rt_ref[...], wrt_ref[...])
    acc = acc + _sigmoid(g2_ref[...].astype(F32)) * _dot(xa_ref[...], wxa_ref[...])
    o_ref[...] = acc.astype(BF16)


def _merge(ml_out, rt_out, xa_out, w_ml, w_rt, w_xa, p, tm, tn):
    t = ml_out.shape[0]
    nj = D_MODEL // tn

    def gspec(br):
        return pl.BlockSpec((tm, tn), lambda i, j, o=(OFF_GATE + br * D_MODEL) // tn: (i, o + j))

    return pl.pallas_call(
        _merge_kernel,
        grid=(t // tm, nj),
        in_specs=[pl.BlockSpec((tm, ML_V), lambda i, j: (i, 0)),
                  pl.BlockSpec((tm, RET_V), lambda i, j: (i, 0)),
                  pl.BlockSpec((tm, XA_W), lambda i, j: (i, 0)),
                  pl.BlockSpec((ML_V, tn), lambda i, j: (0, j)),
                  pl.BlockSpec((RET_V, tn), lambda i, j: (0, j)),
                  pl.BlockSpec((XA_W, tn), lambda i, j: (0, j)),
                  gspec(0), gspec(1), gspec(2)],
        out_specs=pl.BlockSpec((tm, tn), lambda i, j: (i, j)),
        out_shape=jax.ShapeDtypeStruct((t, D_MODEL), BF16),
        compiler_params=_params(("arbitrary", "arbitrary")),
        name="merge",
    )(ml_out, rt_out, xa_out, w_ml, w_rt, w_xa, p, p, p)


def _outproj_kernel(x_ref, m_ref, w_ref, g_ref, o_ref):
    y = x_ref[...] + _dot(m_ref[...], w_ref[...])
    o_ref[...] = _rms(y, g_ref[...])


def _outproj(x2, merged, w_out, final_g, tm):
    t = x2.shape[0]
    return pl.pallas_call(
        _outproj_kernel,
        grid=(t // tm,),
        in_specs=[pl.BlockSpec((tm, D_MODEL), lambda i: (i, 0)),
                  pl.BlockSpec((tm, D_MODEL), lambda i: (i, 0)),
                  pl.BlockSpec((D_MODEL, D_MODEL), lambda i: (0, 0)),
                  pl.BlockSpec((1, D_MODEL), lambda i: (0, 0))],
        out_specs=pl.BlockSpec((tm, D_MODEL), lambda i: (i, 0)),
        out_shape=jax.ShapeDtypeStruct((t, D_MODEL), F32),
        compiler_params=_params(("arbitrary",)),
        name="outproj",
    )(x2, merged, w_out, final_g)


def kernel(x, mem, positions, ln_g, mem_ln_g, w_in, b_in, conv_w, conv_b, ml_hnorm_g, ret_hnorm_g, w_mem_kv,
           w_br_ml, w_br_ret, w_br_xa, w_out, final_g):
    batch, seq, _ = x.shape
    assert seq % CHUNK == 0 and ln_g.shape[0] == 1
    t = batch * seq
    x2 = x.reshape(t, D_MODEL)

    wt, b0 = w_in[0].T, b_in[0]
    w_full = wt.astype(BF16)
    w_main = lax.dynamic_update_slice(w_full, w_full[OFF_IF + 2 * ML_HEADS:], (OFF_IF, 0))
    b_main = jnp.concatenate([b0[:OFF_IF], b0[OFF_IF + 2 * ML_HEADS:]])[None, :]
    wif_t = jnp.zeros((GATE_ROWS, D_MODEL), F32)
    wif_t = wif_t.at[0:ML_HEADS].set(wt[OFF_IF:OFF_IF + ML_HEADS])
    wif_t = wif_t.at[F_ROW:F_ROW + ML_HEADS].set(wt[OFF_IF + ML_HEADS:OFF_IF + 2 * ML_HEADS]).astype(BF16)
    bif_t = jnp.zeros((GATE_ROWS, 1), F32)
    bif_t = bif_t.at[0:ML_HEADS, 0].set(b0[OFF_IF:OFF_IF + ML_HEADS])
    bif_t = bif_t.at[F_ROW:F_ROW + ML_HEADS, 0].set(b0[OFF_IF + ML_HEADS:OFF_IF + 2 * ML_HEADS])

    memkv = _memkv(mem.reshape(batch * MEM_TOKENS, D_MODEL), mem_ln_g[0][None, :], w_mem_kv[0].astype(BF16))

    tm = min(1024, t)
    p, gates_t = _inproj(x2, ln_g[0][None, :], w_main, b_main, wif_t, bif_t, min(2048, t), 1024, 2)
    gates3 = gates_t.reshape(GATE_ROWS, t // CHUNK, CHUNK)

    half = RET_DQK // 2
    freqs = ROPE_BASE ** (-jnp.arange(half, dtype=F32) / half)
    freq2 = jnp.concatenate([freqs, freqs])[None, :]
    sign2 = jnp.concatenate([-jnp.ones((half,), F32), jnp.ones((half,), F32)])[None, :]
    cos2, sin2 = _rope_tables(positions.reshape(t, 1), freq2, sign2)
    log_gamma = jnp.asarray(np.log(1.0 - 2.0 ** (-5.0 - np.arange(RET_HEADS))), dtype=F32)

    ml_out = _mlstm(p, gates3, conv_w[0], conv_b[0][None, :], ml_hnorm_g[0][None, :], batch, seq)
    rt_out = _retention(log_gamma, p, cos2, sin2, ret_hnorm_g[0][None, :], batch, seq)
    xa_out = _xattn(p, memkv, batch, seq)

    merged = _merge(ml_out, rt_out, xa_out, w_br_ml[0].astype(BF16), w_br_ret[0].astype(BF16),
                    w_br_xa[0].astype(BF16), p, tm, 512)
    out = _outproj(x2, merged, w_out[0].astype(BF16), final_g[None, :], min(512, t))
    return out.reshape(batch, seq, D_MODEL)
```

```python
import functools

import numpy as np
import jax
import jax.numpy as jnp
from jax import lax
from jax.experimental import pallas as pl
from jax.experimental.pallas import tpu as pltpu

F32 = jnp.float32
BF16 = jnp.bfloat16

D_MODEL = 2048
MEM_TOKENS = 256
ML_HEADS = 6
ML_DQK = 128
ML_DV = 256
ML_CONV = 4
RET_HEADS = 6
RET_DQK = 128
RET_DV = 256
XA_HEADS = 4
XA_DH = 256
ROPE_BASE = 10000.0
EPS = 1e-6
N_BRANCH = 3

ML_QK = ML_HEADS * ML_DQK
ML_V = ML_HEADS * ML_DV
RET_QK = RET_HEADS * RET_DQK
RET_V = RET_HEADS * RET_DV
XA_W = XA_HEADS * XA_DH

OFF_ML_Q = 0
OFF_ML_K = OFF_ML_Q + ML_QK
OFF_ML_V = OFF_ML_K + ML_QK
OFF_ML_O = OFF_ML_V + ML_V
OFF_ML_Z = OFF_ML_O + ML_V
OFF_IF = OFF_ML_Z + ML_V
OFF_RT_Q = OFF_ML_Z + ML_V
OFF_RT_K = OFF_RT_Q + RET_QK
OFF_RT_V = OFF_RT_K + RET_QK
OFF_RT_Z = OFF_RT_V + RET_V
OFF_XA_Q = OFF_RT_Z + RET_V
OFF_XA_Z = OFF_XA_Q + XA_W
OFF_GATE = OFF_XA_Z + XA_W
N_MAIN = OFF_GATE + N_BRANCH * D_MODEL

CHUNK = 256
GATE_ROWS = 16
F_ROW = 8
CONV_BLOCK = 128

LANES = 128
VMEM_LIMIT = 56 * 1024 * 1024


def _params(sem):
    return pltpu.CompilerParams(dimension_semantics=sem, vmem_limit_bytes=VMEM_LIMIT)


def _sigmoid(x):
    return 0.5 * jnp.tanh(0.5 * x) + 0.5


def _rms(xf, g):
    return xf * lax.rsqrt(jnp.mean(xf * xf, axis=-1, keepdims=True) + EPS) * g


def _head_norm(t, g):
    mu = jnp.mean(t, axis=-1, keepdims=True)
    c = t - mu
    var = jnp.mean(c * c, axis=-1, keepdims=True)
    return c * lax.rsqrt(var + EPS) * g


def _dot(a, b):
    return jnp.dot(a, b, preferred_element_type=F32)


def _memkv_kernel(m_ref, g_ref, w_ref, o_ref):
    h = _rms(m_ref[...], g_ref[...])
    o_ref[...] = _dot(h.astype(BF16), w_ref[...]).astype(BF16)


def _memkv(mem2, g, w):
    rows = mem2.shape[0]
    tm = min(512, rows)
    return pl.pallas_call(
        _memkv_kernel,
        grid=(rows // tm,),
        in_specs=[pl.BlockSpec((tm, D_MODEL), lambda i: (i, 0)),
                  pl.BlockSpec((1, D_MODEL), lambda i: (0, 0)),
                  pl.BlockSpec((D_MODEL, 2 * XA_W), lambda i: (0, 0))],
        out_specs=pl.BlockSpec((tm, 2 * XA_W), lambda i: (i, 0)),
        out_shape=jax.ShapeDtypeStruct((rows, 2 * XA_W), BF16),
        compiler_params=_params(("arbitrary",)),
        name="memkv",
    )(mem2, g, w)


def _inproj_kernel(x_ref, g_ref, w_ref, b_ref, wif_ref, bif_ref, p_ref, gate_ref, hb_ref, *, nsplit, tail):
    j = pl.program_id(1)
    last = pl.num_programs(1) - 1
    tx = x_ref.shape[0]
    tn = w_ref.shape[0]

    def project(width):
        acc = lax.dot_general(hb_ref[...], w_ref[pl.ds(0, width), :], (((1,), (1,)), ((), ())),
                              preferred_element_type=F32)
        p_ref[:, pl.ds(0, width)] = (acc + b_ref[:, pl.ds(0, width)]).astype(BF16)

    @pl.when(j < nsplit)
    def _():
        hb = _rms(x_ref[...], g_ref[...]).astype(BF16)
        r0 = pl.multiple_of(j * tx, tx)
        hb_ref[pl.ds(r0, tx), :] = hb
        gt = lax.dot_general(wif_ref[...], hb, (((1,), (1,)), ((), ())), preferred_element_type=F32)
        gate_ref[:, pl.ds(r0, tx)] = gt + bif_ref[...]

    if tail == tn:
        pl.when(j >= nsplit)(lambda: project(tn))
    else:
        pl.when(jnp.logical_and(j >= nsplit, j < last))(lambda: project(tn))
        pl.when(j == last)(lambda: project(tail))


def _inproj(x2, g, w_main, b_main, wif_t, bif_t, tm, tn, nsplit):
    t = x2.shape[0]
    tx = tm // nsplit
    ncol = pl.cdiv(N_MAIN, tn)
    tail = N_MAIN - (ncol - 1) * tn

    def col(j):
        return jnp.maximum(j - nsplit, 0)

    return pl.pallas_call(
        functools.partial(_inproj_kernel, nsplit=nsplit, tail=tail),
        grid=(t // tm, nsplit + ncol),
        in_specs=[pl.BlockSpec((tx, D_MODEL), lambda i, j: (i * nsplit + jnp.minimum(j, nsplit - 1), 0)),
                  pl.BlockSpec((1, D_MODEL), lambda i, j: (0, 0)),
                  pl.BlockSpec((tn, D_MODEL), lambda i, j: (col(j), 0)),
                  pl.BlockSpec((1, tn), lambda i, j: (0, col(j))),
                  pl.BlockSpec((GATE_ROWS, D_MODEL), lambda i, j: (0, 0)),
                  pl.BlockSpec((GATE_ROWS, 1), lambda i, j: (0, 0))],
        out_specs=[pl.BlockSpec((tm, tn), lambda i, j: (i, col(j))),
                   pl.BlockSpec((GATE_ROWS, tm), lambda i, j: (0, i))],
        out_shape=[jax.ShapeDtypeStruct((t, N_MAIN), BF16),
                   jax.ShapeDtypeStruct((GATE_ROWS, t), F32)],
        scratch_shapes=[pltpu.VMEM((tm, D_MODEL), BF16)],
        compiler_params=_params(("arbitrary", "arbitrary")),
        name="inproj",
    )(x2, g, w_main, b_main, wif_t, bif_t)


def _rope_kernel(pos_ref, freq_ref, sign_ref, cos_ref, sin_ref):
    ang = pos_ref[...].astype(F32) * freq_ref[...]
    cos_ref[...] = jnp.cos(ang)
    sin_ref[...] = jnp.sin(ang) * sign_ref[...]


def _rope_tables(pos_col, freq2, sign2):
    t = pos_col.shape[0]
    tm = min(2048, t)
    return pl.pallas_call(
        _rope_kernel,
        grid=(t // tm,),
        in_specs=[pl.BlockSpec((tm, 1), lambda i: (i, 0)),
                  pl.BlockSpec((1, RET_DQK), lambda i: (0, 0)),
                  pl.BlockSpec((1, RET_DQK), lambda i: (0, 0))],
        out_specs=[pl.BlockSpec((tm, RET_DQK), lambda i: (i, 0)),
                   pl.BlockSpec((tm, RET_DQK), lambda i: (i, 0))],
        out_shape=[jax.ShapeDtypeStruct((t, RET_DQK), F32),
                   jax.ShapeDtypeStruct((t, RET_DQK), F32)],
        compiler_params=_params(("arbitrary",)),
        name="rope",
    )(pos_col, freq2, sign2)


def _split3(x):
    hi = x.astype(BF16)
    r1 = x - hi.astype(F32)
    mid = r1.astype(BF16)
    lo = (r1 - mid.astype(F32)).astype(BF16)
    return hi, mid, lo


def _conv_shift_matrix():
    s = np.zeros(((ML_CONV - 1) * CONV_BLOCK, 2 * CONV_BLOCK), np.float32)
    t = np.arange(CONV_BLOCK)
    for j in range(ML_CONV - 1):
        s[j * CONV_BLOCK + t, CONV_BLOCK + t - (ML_CONV - 1) + j] = 1.0
    return s


def _mlstm_setup(q_ref, k_ref, v_ref, o_ref, z_ref, gi_ref, gf_ref, cwq_ref, cwk_ref, cbq_ref, cbk_ref,
                 hg_ref, shift_ref, out_ref, pad_ref, qc_ref, kc_ref, b_ref, a_ref, w_ref, mp_ref, mn_ref, dec_ref,
                 caug_ref, *, seq):
    L = CHUNK
    nchunk = seq // L

    pad_ref[pl.ds(0, CONV_BLOCK), :] = jnp.zeros((CONV_BLOCK, 2 * ML_DQK), BF16)
    pad_ref[pl.ds(CONV_BLOCK, seq), pl.ds(0, ML_DQK)] = q_ref[...]
    pad_ref[pl.ds(CONV_BLOCK, seq), pl.ds(ML_DQK, ML_DQK)] = k_ref[...]
    cw = jnp.concatenate([cwq_ref[...], cwk_ref[...]], axis=-1)
    cb = jnp.concatenate([cbq_ref[...], cbk_ref[...]], axis=-1)
    shift = shift_ref[...]

    def conv_block(i, carry):
        r = pl.multiple_of(i * CONV_BLOCK, CONV_BLOCK)
        taps = _dot(shift, pad_ref[pl.ds(r, 2 * CONV_BLOCK), :])
        y = cb + pad_ref[pl.ds(r + CONV_BLOCK, CONV_BLOCK), :].astype(F32) * cw[ML_CONV - 1:ML_CONV, :]
        for j in range(ML_CONV - 1):
            y = y + taps[j * CONV_BLOCK:(j + 1) * CONV_BLOCK, :] * cw[j:j + 1, :]
        yh = 0.5 * y
        y = yh * (1.0 + jnp.tanh(yh))
        qc_ref[pl.ds(r, CONV_BLOCK), :] = y[:, :ML_DQK] * (ML_DQK ** -0.5)
        kc_ref[pl.ds(r, CONV_BLOCK), :] = y[:, ML_DQK:]
        return carry

    lax.fori_loop(0, seq // CONV_BLOCK, conv_block, 0, unroll=8)

    f = gf_ref[0]
    lf = jnp.minimum(f, 0.0) - jnp.log1p(jnp.exp(-jnp.abs(f)))
    row = lax.broadcasted_iota(jnp.int32, (L, L), 0)
    col = lax.broadcasted_iota(jnp.int32, (L, L), 1)
    triu = (row <= col).astype(BF16)
    hi, mid, lo = _split3(lf)
    bb = _dot(hi, triu) + _dot(mid, triu) + _dot(lo, triu)
    causal = row >= col
    diag = row == col

    gi = gi_ref[0]
    g_col = bb[:, L - 1:L]
    log_w = g_col - bb + gi
    w_max = jnp.max(log_w, axis=-1, keepdims=True)
    m = jnp.zeros((1, 1), F32)
    for c in range(nchunk):
        mp_ref[c:c + 1, :] = m
        m = jnp.maximum(g_col[c:c + 1, :] + m, w_max[c:c + 1, :])
        mn_ref[c:c + 1, :] = m
    m_prev_all = mp_ref[...]
    m_new_all = mn_ref[...]
    b_ref[...] = bb
    a_ref[...] = gi - bb
    w_ref[...] = jnp.exp(log_w - m_new_all)
    dec_ref[...] = jnp.exp(g_col + m_prev_all - m_new_all)

    caug_ref[...] = jnp.zeros_like(caug_ref)
    hg_half = 0.5 * hg_ref[...]
    ones_blk = jnp.ones((L, LANES), BF16)

    def chunk(c):
        r0 = pl.multiple_of(c * L, L)
        rows = pl.ds(r0, L)
        one = pl.ds(c, 1)
        b_row = b_ref[one, :]
        a_row = a_ref[one, :]
        w_row = w_ref[one, :]
        m_prev = mp_ref[one, :]
        decay = dec_ref[one, :]
        a_mat = jnp.where(causal, a_row, -jnp.inf)
        u = jnp.maximum(m_prev, jnp.max(a_mat, axis=-1, keepdims=True))
        d = jnp.exp(a_mat - u)
        b_col = jnp.sum(jnp.where(diag, b_row, 0.0), axis=-1, keepdims=True)
        inter = jnp.exp(m_prev - u)
        qf = qc_ref[rows, :]
        kf = kc_ref[rows, :]
        qb = qf.astype(BF16)
        kt = kf.T
        vb = v_ref[rows, :]
        s = _dot(qb, kt.astype(BF16)) * d
        caug = caug_ref[...]
        vaug = jnp.concatenate([vb, ones_blk], axis=1)
        na = _dot(s.astype(BF16), vaug) + inter * _dot(qb, caug.astype(BF16))
        den = jnp.maximum(jnp.abs(na[:, ML_DV:]), jnp.exp(-(b_col + u)))
        hh = na[:, :ML_DV] * jnp.tile(1.0 / den, (1, ML_DV // LANES))

        caug_ref[...] = decay * caug + _dot((kt * w_row).astype(BF16), vaug)

        y = _head_norm(hh, hg_half)
        oh = 0.5 * o_ref[rows, :].astype(F32)
        zh = 0.5 * z_ref[rows, :].astype(F32)
        y = y * (1.0 + jnp.tanh(oh)) * (zh * (1.0 + jnp.tanh(zh)))
        out_ref[rows, :] = y.astype(BF16)

    return chunk


def _ret_setup(lg_ref, q_ref, k_ref, v_ref, z_ref, cos_ref, sin_ref, hg_ref, out_ref, r_ref):
    L = CHUNK
    lg = lg_ref[pl.program_id(1)]
    row = lax.broadcasted_iota(jnp.int32, (L, L), 0)
    col = lax.broadcasted_iota(jnp.int32, (L, L), 1)
    intra = jnp.where(row >= col, jnp.exp((row - col).astype(F32) * lg), 0.0)
    pos_col = lax.broadcasted_iota(jnp.int32, (L, 1), 0).astype(F32)
    pos_row = lax.broadcasted_iota(jnp.int32, (1, L), 1).astype(F32)
    q_decay = jnp.exp((pos_col + 1.0) * lg)
    k_decay = jnp.exp((L - 1.0 - pos_row) * lg)
    chunk_decay = jnp.exp(jnp.full((1, 1), float(L), F32) * lg)
    r_ref[...] = jnp.zeros_like(r_ref)
    half = RET_DQK // 2
    hg = hg_ref[...]

    def chunk(c):
        r0 = pl.multiple_of(c * L, L)
        rows = pl.ds(r0, L)
        cs = cos_ref[rows, :]
        sn = sin_ref[rows, :]
        qf = q_ref[rows, :].astype(F32)
        kf = k_ref[rows, :].astype(F32)
        qr = qf * cs + pltpu.roll(qf, half, 1) * sn
        kr = (kf * cs + pltpu.roll(kf, half, 1) * sn) * (RET_DQK ** -0.5)
        qb = qr.astype(BF16)
        kt = kr.T
        vb = v_ref[rows, :]
        s = _dot(qb, kt.astype(BF16)) * intra
        r_state = r_ref[...]
        o = _dot(s.astype(BF16), vb) + q_decay * _dot(qb, r_state.astype(BF16))
        r_ref[...] = chunk_decay * r_state + _dot((kt * k_decay).astype(BF16), vb)
        y = _head_norm(o, hg)
        zh = 0.5 * z_ref[rows, :].astype(F32)
        out_ref[rows, :] = (y * (zh * (1.0 + jnp.tanh(zh)))).astype(BF16)

    return chunk


N_ML_IN = 13
N_RT_IN = 7
N_ML_SCRATCH = 10
CHUNK_UNROLL = 4


def _mixers_kernel(lg_ref, *refs, seq):
    ml_in = refs[:N_ML_IN]
    rt_in = refs[N_ML_IN:N_ML_IN + N_RT_IN]
    ml_out, rt_out = refs[N_ML_IN + N_RT_IN:N_ML_IN + N_RT_IN + 2]
    scratch = refs[N_ML_IN + N_RT_IN + 2:]
    nchunk = seq // CHUNK
    ml_chunk = _mlstm_setup(*ml_in, ml_out, *scratch[:N_ML_SCRATCH], seq=seq)
    lax.fori_loop(0, nchunk, lambda c, carry: (ml_chunk(c), carry)[1], 0, unroll=CHUNK_UNROLL)
    rt_chunk = _ret_setup(lg_ref, *rt_in, rt_out, scratch[N_ML_SCRATCH])
    lax.fori_loop(0, nchunk, lambda c, carry: (rt_chunk(c), carry)[1], 0, unroll=CHUNK_UNROLL)


def _mixers(log_gamma, p, gates3, conv_w, conv_b, ml_g, cos2, sin2, rt_g, batch, seq):
    t = batch * seq
    nchunk = seq // CHUNK
    qb, vb = ML_DQK, ML_DV
    assert (ML_HEADS, ML_DQK, ML_DV) == (RET_HEADS, RET_DQK, RET_DV)

    def pspec(width, off):
        return pl.BlockSpec((seq, width), lambda b, h, lg, off=off // width: (b, off + h))

    def gspec(row0):
        return pl.BlockSpec((1, nchunk, CHUNK), lambda b, h, lg, row0=row0: (row0 + h, b, 0))

    ml_specs = [pspec(qb, OFF_ML_Q), pspec(qb, OFF_ML_K), pspec(vb, OFF_ML_V), pspec(vb, OFF_ML_O),
                pspec(vb, OFF_ML_Z), gspec(0), gspec(F_ROW),
                pl.BlockSpec((ML_CONV, qb), lambda b, h, lg: (0, h)),
                pl.BlockSpec((ML_CONV, qb), lambda b, h, lg: (0, ML_HEADS + h)),
                pl.BlockSpec((1, qb), lambda b, h, lg: (0, h)),
                pl.BlockSpec((1, qb), lambda b, h, lg: (0, ML_HEADS + h)),
                pl.BlockSpec((1, vb), lambda b, h, lg: (0, h)),
                pl.BlockSpec(((ML_CONV - 1) * CONV_BLOCK, 2 * CONV_BLOCK), lambda b, h, lg: (0, 0))]
    rt_specs = [pspec(qb, OFF_RT_Q), pspec(qb, OFF_RT_K), pspec(vb, OFF_RT_V), pspec(vb, OFF_RT_Z),
                pl.BlockSpec((seq, qb), lambda b, h, lg: (b, 0)),
                pl.BlockSpec((seq, qb), lambda b, h, lg: (b, 0)),
                pl.BlockSpec((1, vb), lambda b, h, lg: (0, h))]
    assert len(ml_specs) == N_ML_IN and len(rt_specs) == N_RT_IN
    ml_scratch = [pltpu.VMEM((seq + CONV_BLOCK, 2 * qb), BF16),
                  pltpu.VMEM((seq, qb), F32),
                  pltpu.VMEM((seq, qb), F32),
                  pltpu.VMEM((nchunk, CHUNK), F32),
                  pltpu.VMEM((nchunk, CHUNK), F32),
                  pltpu.VMEM((nchunk, CHUNK), F32),
                  pltpu.VMEM((nchunk, 1), F32),
                  pltpu.VMEM((nchunk, 1), F32),
                  pltpu.VMEM((nchunk, 1), F32),
                  pltpu.VMEM((qb, vb + LANES), F32)]
    assert len(ml_scratch) == N_ML_SCRATCH
    out_spec = pl.BlockSpec((seq, vb), lambda b, h, lg: (b, h))
    grid_spec = pltpu.PrefetchScalarGridSpec(
        num_scalar_prefetch=1,
        grid=(batch, ML_HEADS),
        in_specs=ml_specs + rt_specs,
        out_specs=[out_spec, out_spec],
        scratch_shapes=ml_scratch + [pltpu.VMEM((qb, vb), F32)],
    )
    return pl.pallas_call(
        functools.partial(_mixers_kernel, seq=seq),
        grid_spec=grid_spec,
        out_shape=[jax.ShapeDtypeStruct((t, ML_V), BF16), jax.ShapeDtypeStruct((t, RET_V), BF16)],
        compiler_params=_params(("arbitrary", "arbitrary")),
        name="mixers",
    )(log_gamma, p, p, p, p, p, gates3, gates3, conv_w, conv_w, conv_b, conv_b, ml_g,
      jnp.asarray(_conv_shift_matrix(), BF16), p, p, p, p, cos2, sin2, rt_g)


def _xattn_kernel(q_ref, z_ref, mk_ref, mv_ref, out_ref, *, blk):
    mk = mk_ref[...]
    mv = mv_ref[...]

    def block(i, carry):
        rows = pl.ds(pl.multiple_of(i * blk, blk), blk)
        sc = lax.dot_general(q_ref[rows, :], mk, (((1,), (1,)), ((), ())), preferred_element_type=F32)
        sc = sc * (XA_DH ** -0.5)
        e = jnp.exp(sc - jnp.max(sc, axis=-1, keepdims=True))
        o = _dot(e.astype(BF16), mv) * (1.0 / jnp.sum(e, axis=-1, keepdims=True))
        zh = 0.5 * z_ref[rows, :].astype(F32)
        out_ref[rows, :] = (o * (zh * (1.0 + jnp.tanh(zh)))).astype(BF16)
        return carry

    lax.fori_loop(0, q_ref.shape[0] // blk, block, 0, unroll=2)


def _xattn(p, memkv, batch, seq):
    t = batch * seq
    ts = seq
    ns = seq // ts
    w = XA_DH
    return pl.pallas_call(
        functools.partial(_xattn_kernel, blk=min(1024, ts)),
        grid=(batch, XA_HEADS, ns),
        in_specs=[pl.BlockSpec((ts, w), lambda b, h, s: (b * ns + s, OFF_XA_Q // w + h)),
                  pl.BlockSpec((ts, w), lambda b, h, s: (b * ns + s, OFF_XA_Z // w + h)),
                  pl.BlockSpec((MEM_TOKENS, w), lambda b, h, s: (b, h)),
                  pl.BlockSpec((MEM_TOKENS, w), lambda b, h, s: (b, XA_HEADS + h))],
        out_specs=pl.BlockSpec((ts, w), lambda b, h, s: (b * ns + s, h)),
        out_shape=jax.ShapeDtypeStruct((t, XA_W), BF16),
        compiler_params=_params(("arbitrary", "arbitrary", "arbitrary")),
        name="xattn",
    )(p, p, memkv, memkv)


def _merge_kernel(ml_ref, rt_ref, xa_ref, wml_ref, wrt_ref, wxa_ref, g0_ref, g1_ref, g2_ref, o_ref):
    acc = _sigmoid(g0_ref[...].astype(F32)) * _dot(ml_ref[...], wml_ref[...])
    acc = acc + _sigmoid(g1_ref[...].astype(F32)) * _dot(rt_ref[...], wrt_ref[...])
    acc = acc + _sigmoid(g2_ref[...].astype(F32)) * _dot(xa_ref[...], wxa_ref[...])
    o_ref[...] = acc.astype(BF16)


def _merge(ml_out, rt_out, xa_out, w_ml, w_rt, w_xa, p, tm, tn):
    t = ml_out.shape[0]
    nj = D_MODEL // tn

    def gspec(br):
        return pl.BlockSpec((tm, tn), lambda i, j, o=(OFF_GATE + br * D_MODEL) // tn: (i, o + j))

    return pl.pallas_call(
        _merge_kernel,
        grid=(t // tm, nj),
        in_specs=[pl.BlockSpec((tm, ML_V), lambda i, j: (i, 0)),
                  pl.BlockSpec((tm, RET_V), lambda i, j: (i, 0)),
                  pl.BlockSpec((tm, XA_W), lambda i, j: (i, 0)),
                  pl.BlockSpec((ML_V, tn), lambda i, j: (0, j)),
                  pl.BlockSpec((RET_V, tn), lambda i, j: (0, j)),
                  pl.BlockSpec((XA_W, tn), lambda i, j: (0, j)),
                  gspec(0), gspec(1), gspec(2)],
        out_specs=pl.BlockSpec((tm, tn), lambda i, j: (i, j)),
        out_shape=jax.ShapeDtypeStruct((t, D_MODEL), BF16),
        compiler_params=_params(("arbitrary", "arbitrary")),
        name="merge",
    )(ml_out, rt_out, xa_out, w_ml, w_rt, w_xa, p, p, p)


def _outproj_kernel(x_ref, m_ref, w_ref, g_ref, o_ref):
    y = x_ref[...] + _dot(m_ref[...], w_ref[...])
    o_ref[...] = _rms(y, g_ref[...])


def _outproj(x2, merged, w_out, final_g, tm):
    t = x2.shape[0]
    return pl.pallas_call(
        _outproj_kernel,
        grid=(t // tm,),
        in_specs=[pl.BlockSpec((tm, D_MODEL), lambda i: (i, 0)),
                  pl.BlockSpec((tm, D_MODEL), lambda i: (i, 0)),
                  pl.BlockSpec((D_MODEL, D_MODEL), lambda i: (0, 0)),
                  pl.BlockSpec((1, D_MODEL), lambda i: (0, 0))],
        out_specs=pl.BlockSpec((tm, D_MODEL), lambda i: (i, 0)),
        out_shape=jax.ShapeDtypeStruct((t, D_MODEL), F32),
        compiler_params=_params(("arbitrary",)),
        name="outproj",
    )(x2, merged, w_out, final_g)


def kernel(x, mem, positions, ln_g, mem_ln_g, w_in, b_in, conv_w, conv_b, ml_hnorm_g, ret_hnorm_g, w_mem_kv,
           w_br_ml, w_br_ret, w_br_xa, w_out, final_g):
    batch, seq, _ = x.shape
    assert seq % CHUNK == 0 and ln_g.shape[0] == 1
    t = batch * seq
    x2 = x.reshape(t, D_MODEL)

    wt, b0 = w_in[0].T, b_in[0]
    w_full = wt.astype(BF16)
    w_main = lax.dynamic_update_slice(w_full, w_full[OFF_IF + 2 * ML_HEADS:], (OFF_IF, 0))
    b_main = jnp.concatenate([b0[:OFF_IF], b0[OFF_IF + 2 * ML_HEADS:]])[None, :]
    wif_t = jnp.zeros((GATE_ROWS, D_MODEL), F32)
    wif_t = wif_t.at[0:ML_HEADS].set(wt[OFF_IF:OFF_IF + ML_HEADS])
    wif_t = wif_t.at[F_ROW:F_ROW + ML_HEADS].set(wt[OFF_IF + ML_HEADS:OFF_IF + 2 * ML_HEADS]).astype(BF16)
    bif_t = jnp.zeros((GATE_ROWS, 1), F32)
    bif_t = bif_t.at[0:ML_HEADS, 0].set(b0[OFF_IF:OFF_IF + ML_HEADS])
    bif_t = bif_t.at[F_ROW:F_ROW + ML_HEADS, 0].set(b0[OFF_IF + ML_HEADS:OFF_IF + 2 * ML_HEADS])

    memkv = _memkv(mem.reshape(batch * MEM_TOKENS, D_MODEL), mem_ln_g[0][None, :], w_mem_kv[0].astype(BF16))

    tm = min(1024, t)
    p, gates_t = _inproj(x2, ln_g[0][None, :], w_main, b_main, wif_t, bif_t, min(2048, t), 1024, 2)
    gates3 = gates_t.reshape(GATE_ROWS, t // CHUNK, CHUNK)

    half = RET_DQK // 2
    freqs = ROPE_BASE ** (-jnp.arange(half, dtype=F32) / half)
    freq2 = jnp.concatenate([freqs, freqs])[None, :]
    sign2 = jnp.concatenate([-jnp.ones((half,), F32), jnp.ones((half,), F32)])[None, :]
    cos2, sin2 = _rope_tables(positions.reshape(t, 1), freq2, sign2)
    log_gamma = jnp.asarray(np.log(1.0 - 2.0 ** (-5.0 - np.arange(RET_HEADS))), dtype=F32)

    ml_out, rt_out = _mixers(log_gamma, p, gates3, conv_w[0], conv_b[0][None, :], ml_hnorm_g[0][None, :],
                             cos2, sin2, ret_hnorm_g[0][None, :], batch, seq)
    xa_out = _xattn(p, memkv, batch, seq)

    merged = _merge(ml_out, rt_out, xa_out, w_br_ml[0].astype(BF16), w_br_ret[0].astype(BF16),
                    w_br_xa[0].astype(BF16), p, tm, 512)
    out = _outproj(x2, merged, w_out[0].astype(BF16), final_g[None, :], min(512, t))
    return out.reshape(batch, seq, D_MODEL)
```

```python
import functools

import numpy as np
import jax
import jax.numpy as jnp
from jax import lax
from jax.experimental import pallas as pl
from jax.experimental.pallas import tpu as pltpu

F32 = jnp.float32
BF16 = jnp.bfloat16

D_MODEL = 2048
MEM_TOKENS = 256
ML_HEADS = 6
ML_DQK = 128
ML_DV = 256
ML_CONV = 4
RET_HEADS = 6
RET_DQK = 128
RET_DV = 256
XA_HEADS = 4
XA_DH = 256
ROPE_BASE = 10000.0
EPS = 1e-6
N_BRANCH = 3

ML_QK = ML_HEADS * ML_DQK
ML_V = ML_HEADS * ML_DV
RET_QK = RET_HEADS * RET_DQK
RET_V = RET_HEADS * RET_DV
XA_W = XA_HEADS * XA_DH

OFF_ML_Q = 0
OFF_ML_K = OFF_ML_Q + ML_QK
OFF_ML_V = OFF_ML_K + ML_QK
OFF_ML_O = OFF_ML_V + ML_V
OFF_ML_Z = OFF_ML_O + ML_V
OFF_IF = OFF_ML_Z + ML_V
OFF_RT_Q = OFF_ML_Z + ML_V
OFF_RT_K = OFF_RT_Q + RET_QK
OFF_RT_V = OFF_RT_K + RET_QK
OFF_RT_Z = OFF_RT_V + RET_V
OFF_XA_Q = OFF_RT_Z + RET_V
OFF_XA_Z = OFF_XA_Q + XA_W
OFF_GATE = OFF_XA_Z + XA_W
N_MAIN = OFF_GATE + N_BRANCH * D_MODEL

CHUNK = 256
GATE_ROWS = 16
F_ROW = 8
CONV_BLOCK = 128

LANES = 128
VMEM_LIMIT = 56 * 1024 * 1024


def _params(sem):
    return pltpu.CompilerParams(dimension_semantics=sem, vmem_limit_bytes=VMEM_LIMIT)


def _sigmoid(x):
    return 0.5 * jnp.tanh(0.5 * x) + 0.5


def _rms(xf, g):
    return xf * lax.rsqrt(jnp.mean(xf * xf, axis=-1, keepdims=True) + EPS) * g


def _head_norm(t, g):
    mu = jnp.mean(t, axis=-1, keepdims=True)
    c = t - mu
    var = jnp.mean(c * c, axis=-1, keepdims=True)
    return c * lax.rsqrt(var + EPS) * g


def _dot(a, b):
    return jnp.dot(a, b, preferred_element_type=F32)


def _memkv_kernel(m_ref, g_ref, w_ref, o_ref):
    h = _rms(m_ref[...], g_ref[...])
    o_ref[...] = _dot(h.astype(BF16), w_ref[...]).astype(BF16)


def _memkv(mem2, g, w):
    rows = mem2.shape[0]
    tm = min(512, rows)
    return pl.pallas_call(
        _memkv_kernel,
        grid=(rows // tm,),
        in_specs=[pl.BlockSpec((tm, D_MODEL), lambda i: (i, 0)),
                  pl.BlockSpec((1, D_MODEL), lambda i: (0, 0)),
                  pl.BlockSpec((D_MODEL, 2 * XA_W), lambda i: (0, 0))],
        out_specs=pl.BlockSpec((tm, 2 * XA_W), lambda i: (i, 0)),
        out_shape=jax.ShapeDtypeStruct((rows, 2 * XA_W), BF16),
        compiler_params=_params(("arbitrary",)),
        name="memkv",
    )(mem2, g, w)


def _inproj_kernel(x_ref, g_ref, w_ref, b_ref, wif_ref, bif_ref, p_ref, gate_ref, hb_ref, *, nsplit, tail):
    j = pl.program_id(1)
    last = pl.num_programs(1) - 1
    tx = x_ref.shape[0]
    tn = w_ref.shape[0]

    def project(width):
        acc = lax.dot_general(hb_ref[...], w_ref[pl.ds(0, width), :], (((1,), (1,)), ((), ())),
                              preferred_element_type=F32)
        p_ref[:, pl.ds(0, width)] = (acc + b_ref[:, pl.ds(0, width)]).astype(BF16)

    @pl.when(j < nsplit)
    def _():
        hb = _rms(x_ref[...], g_ref[...]).astype(BF16)
        r0 = pl.multiple_of(j * tx, tx)
        hb_ref[pl.ds(r0, tx), :] = hb
        gt = lax.dot_general(wif_ref[...], hb, (((1,), (1,)), ((), ())), preferred_element_type=F32)
        gate_ref[:, pl.ds(r0, tx)] = gt + bif_ref[...]

    if tail == tn:
        pl.when(j >= nsplit)(lambda: project(tn))
    else:
        pl.when(jnp.logical_and(j >= nsplit, j < last))(lambda: project(tn))
        pl.when(j == last)(lambda: project(tail))


def _inproj(x2, g, w_main, b_main, wif_t, bif_t, tm, tn, nsplit):
    t = x2.shape[0]
    tx = tm // nsplit
    ncol = pl.cdiv(N_MAIN, tn)
    tail = N_MAIN - (ncol - 1) * tn

    def col(j):
        return jnp.maximum(j - nsplit, 0)

    return pl.pallas_call(
        functools.partial(_inproj_kernel, nsplit=nsplit, tail=tail),
        grid=(t // tm, nsplit + ncol),
        in_specs=[pl.BlockSpec((tx, D_MODEL), lambda i, j: (i * nsplit + jnp.minimum(j, nsplit - 1), 0)),
                  pl.BlockSpec((1, D_MODEL), lambda i, j: (0, 0)),
                  pl.BlockSpec((tn, D_MODEL), lambda i, j: (col(j), 0)),
                  pl.BlockSpec((1, tn), lambda i, j: (0, col(j))),
                  pl.BlockSpec((GATE_ROWS, D_MODEL), lambda i, j: (0, 0)),
                  pl.BlockSpec((GATE_ROWS, 1), lambda i, j: (0, 0))],
        out_specs=[pl.BlockSpec((tm, tn), lambda i, j: (i, col(j))),
                   pl.BlockSpec((GATE_ROWS, tm), lambda i, j: (0, i))],
        out_shape=[jax.ShapeDtypeStruct((t, N_MAIN), BF16),
                   jax.ShapeDtypeStruct((GATE_ROWS, t), F32)],
        scratch_shapes=[pltpu.VMEM((tm, D_MODEL), BF16)],
        compiler_params=_params(("arbitrary", "arbitrary")),
        name="inproj",
    )(x2, g, w_main, b_main, wif_t, bif_t)


def _rope_kernel(pos_ref, freq_ref, sign_ref, cos_ref, sin_ref):
    ang = pos_ref[...].astype(F32) * freq_ref[...]
    cos_ref[...] = jnp.cos(ang)
    sin_ref[...] = jnp.sin(ang) * sign_ref[...]


def _rope_tables(pos_col, freq2, sign2):
    t = pos_col.shape[0]
    tm = min(2048, t)
    return pl.pallas_call(
        _rope_kernel,
        grid=(t // tm,),
        in_specs=[pl.BlockSpec((tm, 1), lambda i: (i, 0)),
                  pl.BlockSpec((1, RET_DQK), lambda i: (0, 0)),
                  pl.BlockSpec((1, RET_DQK), lambda i: (0, 0))],
        out_specs=[pl.BlockSpec((tm, RET_DQK), lambda i: (i, 0)),
                   pl.BlockSpec((tm, RET_DQK), lambda i: (i, 0))],
        out_shape=[jax.ShapeDtypeStruct((t, RET_DQK), F32),
                   jax.ShapeDtypeStruct((t, RET_DQK), F32)],
        compiler_params=_params(("arbitrary",)),
        name="rope",
    )(pos_col, freq2, sign2)


def _split3(x):
    hi = x.astype(BF16)
    r1 = x - hi.astype(F32)
    mid = r1.astype(BF16)
    lo = (r1 - mid.astype(F32)).astype(BF16)
    return hi, mid, lo


def _conv_shift_matrix():
    s = np.zeros(((ML_CONV - 1) * CONV_BLOCK, 2 * CONV_BLOCK), np.float32)
    t = np.arange(CONV_BLOCK)
    for j in range(ML_CONV - 1):
        s[j * CONV_BLOCK + t, CONV_BLOCK + t - (ML_CONV - 1) + j] = 1.0
    return s


def _mlstm_setup(q_ref, k_ref, v_ref, o_ref, z_ref, gi_ref, gf_ref, cwq_ref, cwk_ref, cbq_ref, cbk_ref,
                 hg_ref, shift_ref, out_ref, pad_ref, qc_ref, kc_ref, u_ref, mr_ref, a_ref, w_ref, mp_ref, mn_ref,
                 dec_ref, caug_ref, *, seq):
    L = CHUNK
    nchunk = seq // L

    pad_ref[pl.ds(0, CONV_BLOCK), :] = jnp.zeros((CONV_BLOCK, 2 * ML_DQK), BF16)
    pad_ref[pl.ds(CONV_BLOCK, seq), pl.ds(0, ML_DQK)] = q_ref[...]
    pad_ref[pl.ds(CONV_BLOCK, seq), pl.ds(ML_DQK, ML_DQK)] = k_ref[...]
    cw = jnp.concatenate([cwq_ref[...], cwk_ref[...]], axis=-1)
    cb = jnp.concatenate([cbq_ref[...], cbk_ref[...]], axis=-1)
    shift = shift_ref[...]

    def conv_block(i, carry):
        r = pl.multiple_of(i * CONV_BLOCK, CONV_BLOCK)
        taps = _dot(shift, pad_ref[pl.ds(r, 2 * CONV_BLOCK), :])
        y = cb + pad_ref[pl.ds(r + CONV_BLOCK, CONV_BLOCK), :].astype(F32) * cw[ML_CONV - 1:ML_CONV, :]
        for j in range(ML_CONV - 1):
            y = y + taps[j * CONV_BLOCK:(j + 1) * CONV_BLOCK, :] * cw[j:j + 1, :]
        yh = 0.5 * y
        y = yh * (1.0 + jnp.tanh(yh))
        qc_ref[pl.ds(r, CONV_BLOCK), :] = y[:, :ML_DQK] * (ML_DQK ** -0.5)
        kc_ref[pl.ds(r, CONV_BLOCK), :] = y[:, ML_DQK:]
        return carry

    lax.fori_loop(0, seq // CONV_BLOCK, conv_block, 0, unroll=8)

    f = gf_ref[0]
    lf = jnp.minimum(f, 0.0) - jnp.log1p(jnp.exp(-jnp.abs(f)))
    row = lax.broadcasted_iota(jnp.int32, (L, L), 0)
    col = lax.broadcasted_iota(jnp.int32, (L, L), 1)
    triu = (row <= col).astype(BF16)
    hi, mid, lo = _split3(lf)
    bb = _dot(hi, triu) + _dot(mid, triu) + _dot(lo, triu)
    causal = row >= col
    diag = row == col

    gi = gi_ref[0]
    g_col = bb[:, L - 1:L]
    log_w = g_col - bb + gi
    w_max = jnp.max(log_w, axis=-1, keepdims=True)
    m = jnp.zeros((1, 1), F32)
    for c in range(nchunk):
        mp_ref[c:c + 1, :] = m
        m = jnp.maximum(g_col[c:c + 1, :] + m, w_max[c:c + 1, :])
        mn_ref[c:c + 1, :] = m
    m_prev_all = mp_ref[...]
    m_new_all = mn_ref[...]
    a_all = gi - bb
    a_ref[...] = a_all
    w_ref[...] = jnp.exp(log_w - m_new_all)
    dec_ref[...] = jnp.exp(g_col + m_prev_all - m_new_all)
    lane = lax.broadcasted_iota(jnp.int32, a_all.shape, 1)
    run = a_all
    shift_by = 1
    while shift_by < L:
        run = jnp.maximum(run, jnp.where(lane >= shift_by, pltpu.roll(run, shift_by, 1), -jnp.inf))
        shift_by *= 2
    u_all = jnp.maximum(m_prev_all, run)
    u_ref[...] = u_all
    mr_ref[...] = bb + u_all
    eye3 = jnp.tile(diag.astype(BF16), (1, 3))

    caug_ref[...] = jnp.zeros_like(caug_ref)
    hg_half = 0.5 * hg_ref[...]
    ones_blk = jnp.ones((L, LANES), BF16)

    def chunk(c):
        r0 = pl.multiple_of(c * L, L)
        rows = pl.ds(r0, L)
        one = pl.ds(c, 1)
        a_row = a_ref[one, :]
        w_row = w_ref[one, :]
        m_prev = mp_ref[one, :]
        decay = dec_ref[one, :]
        u_pieces = jnp.concatenate(_split3(u_ref[one, :]), axis=1)
        m_pieces = jnp.concatenate(_split3(mr_ref[one, :]), axis=1)
        rows_t = jnp.concatenate([jnp.broadcast_to(u_pieces, (LANES, 3 * L)),
                                  jnp.broadcast_to(m_pieces, (LANES, 3 * L))], axis=0)
        um = lax.dot_general(eye3, rows_t, (((1,), (1,)), ((), ())), preferred_element_type=F32)
        u_rep = um[:, :LANES]
        d = jnp.exp(jnp.where(causal, a_row, -jnp.inf) - jnp.tile(u_rep, (1, L // LANES)))
        inter = jnp.exp(m_prev - u_rep)
        qf = qc_ref[rows, :]
        kf = kc_ref[rows, :]
        qb = qf.astype(BF16)
        kt = kf.T
        vb = v_ref[rows, :]
        s = _dot(qb, kt.astype(BF16)) * d
        caug = caug_ref[...]
        vaug = jnp.concatenate([vb, ones_blk], axis=1)
        na = _dot(s.astype(BF16), vaug) + jnp.tile(inter, (1, ML_DV // LANES + 1)) * _dot(qb, caug.astype(BF16))
        den = jnp.maximum(jnp.abs(na[:, ML_DV:]), jnp.exp(-um[:, LANES:]))

        caug_ref[...] = decay * caug + _dot((kt * w_row).astype(BF16), vaug)

        num = na[:, :ML_DV]
        cen = num - jnp.mean(num, axis=-1, keepdims=True)
        var = jnp.mean(cen * cen, axis=-1, keepdims=True)
        y = cen * jnp.tile(lax.rsqrt(var + EPS * (den * den)), (1, ML_DV // LANES)) * hg_half
        oh = 0.5 * o_ref[rows, :].astype(F32)
        zh = 0.5 * z_ref[rows, :].astype(F32)
        y = y * (1.0 + jnp.tanh(oh)) * (zh * (1.0 + jnp.tanh(zh)))
        out_ref[rows, :] = y.astype(BF16)

    return chunk


def _ret_setup(lg_ref, q_ref, k_ref, v_ref, z_ref, cos_ref, sin_ref, hg_ref, out_ref, r_ref):
    L = CHUNK
    lg = lg_ref[pl.program_id(1)]
    row = lax.broadcasted_iota(jnp.int32, (L, L), 0)
    col = lax.broadcasted_iota(jnp.int32, (L, L), 1)
    intra = jnp.where(row >= col, jnp.exp((row - col).astype(F32) * lg), 0.0)
    pos_col = lax.broadcasted_iota(jnp.int32, (L, 1), 0).astype(F32)
    pos_row = lax.broadcasted_iota(jnp.int32, (1, L), 1).astype(F32)
    q_decay = jnp.exp((pos_col + 1.0) * lg)
    k_decay = jnp.exp((L - 1.0 - pos_row) * lg)
    chunk_decay = jnp.exp(jnp.full((1, 1), float(L), F32) * lg)
    r_ref[...] = jnp.zeros_like(r_ref)
    half = RET_DQK // 2
    hg = hg_ref[...]

    def chunk(c):
        r0 = pl.multiple_of(c * L, L)
        rows = pl.ds(r0, L)
        cs = cos_ref[rows, :]
        sn = sin_ref[rows, :]
        qf = q_ref[rows, :].astype(F32)
        kf = k_ref[rows, :].astype(F32)
        qr = qf * cs + pltpu.roll(qf, half, 1) * sn
        kr = (kf * cs + pltpu.roll(kf, half, 1) * sn) * (RET_DQK ** -0.5)
        qb = qr.astype(BF16)
        kt = kr.T
        vb = v_ref[rows, :]
        s = _dot(qb, kt.astype(BF16)) * intra
        r_state = r_ref[...]
        o = _dot(s.astype(BF16), vb) + q_decay * _dot(qb, r_state.astype(BF16))
        r_ref[...] = chunk_decay * r_state + _dot((kt * k_decay).astype(BF16), vb)
        y = _head_norm(o, hg)
        zh = 0.5 * z_ref[rows, :].astype(F32)
        out_ref[rows, :] = (y * (zh * (1.0 + jnp.tanh(zh)))).astype(BF16)

    return chunk


N_ML_IN = 13
N_RT_IN = 7
N_ML_SCRATCH = 11
CHUNK_UNROLL = 4


def _mixers_kernel(lg_ref, *refs, seq):
    ml_in = refs[:N_ML_IN]
    rt_in = refs[N_ML_IN:N_ML_IN + N_RT_IN]
    ml_out, rt_out = refs[N_ML_IN + N_RT_IN:N_ML_IN + N_RT_IN + 2]
    scratch = refs[N_ML_IN + N_RT_IN + 2:]
    nchunk = seq // CHUNK
    ml_chunk = _mlstm_setup(*ml_in, ml_out, *scratch[:N_ML_SCRATCH], seq=seq)
    lax.fori_loop(0, nchunk, lambda c, carry: (ml_chunk(c), carry)[1], 0, unroll=CHUNK_UNROLL)
    rt_chunk = _ret_setup(lg_ref, *rt_in, rt_out, scratch[N_ML_SCRATCH])
    lax.fori_loop(0, nchunk, lambda c, carry: (rt_chunk(c), carry)[1], 0, unroll=CHUNK_UNROLL)


def _mixers(log_gamma, p, gates3, conv_w, conv_b, ml_g, cos2, sin2, rt_g, batch, seq):
    t = batch * seq
    nchunk = seq // CHUNK
    qb, vb = ML_DQK, ML_DV
    assert (ML_HEADS, ML_DQK, ML_DV) == (RET_HEADS, RET_DQK, RET_DV)

    def pspec(width, off):
        return pl.BlockSpec((seq, width), lambda b, h, lg, off=off // width: (b, off + h))

    def gspec(row0):
        return pl.BlockSpec((1, nchunk, CHUNK), lambda b, h, lg, row0=row0: (row0 + h, b, 0))

    ml_specs = [pspec(qb, OFF_ML_Q), pspec(qb, OFF_ML_K), pspec(vb, OFF_ML_V), pspec(vb, OFF_ML_O),
                pspec(vb, OFF_ML_Z), gspec(0), gspec(F_ROW),
                pl.BlockSpec((ML_CONV, qb), lambda b, h, lg: (0, h)),
                pl.BlockSpec((ML_CONV, qb), lambda b, h, lg: (0, ML_HEADS + h)),
                pl.BlockSpec((1, qb), lambda b, h, lg: (0, h)),
                pl.BlockSpec((1, qb), lambda b, h, lg: (0, ML_HEADS + h)),
                pl.BlockSpec((1, vb), lambda b, h, lg: (0, h)),
                pl.BlockSpec(((ML_CONV - 1) * CONV_BLOCK, 2 * CONV_BLOCK), lambda b, h, lg: (0, 0))]
    rt_specs = [pspec(qb, OFF_RT_Q), pspec(qb, OFF_RT_K), pspec(vb, OFF_RT_V), pspec(vb, OFF_RT_Z),
                pl.BlockSpec((seq, qb), lambda b, h, lg: (b, 0)),
                pl.BlockSpec((seq, qb), lambda b, h, lg: (b, 0)),
                pl.BlockSpec((1, vb), lambda b, h, lg: (0, h))]
    assert len(ml_specs) == N_ML_IN and len(rt_specs) == N_RT_IN
    ml_scratch = [pltpu.VMEM((seq + CONV_BLOCK, 2 * qb), BF16),
                  pltpu.VMEM((seq, qb), F32),
                  pltpu.VMEM((seq, qb), F32),
                  pltpu.VMEM((nchunk, CHUNK), F32),
                  pltpu.VMEM((nchunk, CHUNK), F32),
                  pltpu.VMEM((nchunk, CHUNK), F32),
                  pltpu.VMEM((nchunk, CHUNK), F32),
                  pltpu.VMEM((nchunk, 1), F32),
                  pltpu.VMEM((nchunk, 1), F32),
                  pltpu.VMEM((nchunk, 1), F32),
                  pltpu.VMEM((qb, vb + LANES), F32)]
    assert len(ml_scratch) == N_ML_SCRATCH
    out_spec = pl.BlockSpec((seq, vb), lambda b, h, lg: (b, h))
    grid_spec = pltpu.PrefetchScalarGridSpec(
        num_scalar_prefetch=1,
        grid=(batch, ML_HEADS),
        in_specs=ml_specs + rt_specs,
        out_specs=[out_spec, out_spec],
        scratch_shapes=ml_scratch + [pltpu.VMEM((qb, vb), F32)],
    )
    return pl.pallas_call(
        functools.partial(_mixers_kernel, seq=seq),
        grid_spec=grid_spec,
        out_shape=[jax.ShapeDtypeStruct((t, ML_V), BF16), jax.ShapeDtypeStruct((t, RET_V), BF16)],
        compiler_params=_params(("arbitrary", "arbitrary")),
        name="mixers",
    )(log_gamma, p, p, p, p, p, gates3, gates3, conv_w, conv_w, conv_b, conv_b, ml_g,
      jnp.asarray(_conv_shift_matrix(), BF16), p, p, p, p, cos2, sin2, rt_g)


def _xattn_kernel(q_ref, z_ref, mk_ref, mv_ref, out_ref, *, blk):
    mk = mk_ref[...]
    mv = mv_ref[...]

    def block(i, carry):
        rows = pl.ds(pl.multiple_of(i * blk, blk), blk)
        sc = lax.dot_general(q_ref[rows, :], mk, (((1,), (1,)), ((), ())), preferred_element_type=F32)
        sc = sc * (XA_DH ** -0.5)
        e = jnp.exp(sc - jnp.max(sc, axis=-1, keepdims=True))
        o = _dot(e.astype(BF16), mv) * (1.0 / jnp.sum(e, axis=-1, keepdims=True))
        zh = 0.5 * z_ref[rows, :].astype(F32)
        out_ref[rows, :] = (o * (zh * (1.0 + jnp.tanh(zh)))).astype(BF16)
        return carry

    lax.fori_loop(0, q_ref.shape[0] // blk, block, 0, unroll=2)


def _xattn(p, memkv, batch, seq):
    t = batch * seq
    ts = seq
    ns = seq // ts
    w = XA_DH
    return pl.pallas_call(
        functools.partial(_xattn_kernel, blk=min(1024, ts)),
        grid=(batch, XA_HEADS, ns),
        in_specs=[pl.BlockSpec((ts, w), lambda b, h, s: (b * ns + s, OFF_XA_Q // w + h)),
                  pl.BlockSpec((ts, w), lambda b, h, s: (b * ns + s, OFF_XA_Z // w + h)),
                  pl.BlockSpec((MEM_TOKENS, w), lambda b, h, s: (b, h)),
                  pl.BlockSpec((MEM_TOKENS, w), lambda b, h, s: (b, XA_HEADS + h))],
        out_specs=pl.BlockSpec((ts, w), lambda b, h, s: (b * ns + s, h)),
        out_shape=jax.ShapeDtypeStruct((t, XA_W), BF16),
        compiler_params=_params(("arbitrary", "arbitrary", "arbitrary")),
        name="xattn",
    )(p, p, memkv, memkv)


def _merge_kernel(ml_ref, rt_ref, xa_ref, wml_ref, wrt_ref, wxa_ref, g0_ref, g1_ref, g2_ref, o_ref):
    acc = _sigmoid(g0_ref[...].astype(F32)) * _dot(ml_ref[...], wml_ref[...])
    acc = acc + _sigmoid(g1_ref[...].astype(F32)) * _dot(rt_ref[...], wrt_ref[...])
    acc = acc + _sigmoid(g2_ref[...].astype(F32)) * _dot(xa_ref[...], wxa_ref[...])
    o_ref[...] = acc.astype(BF16)


def _merge(ml_out, rt_out, xa_out, w_ml, w_rt, w_xa, p, tm, tn):
    t = ml_out.shape[0]
    nj = D_MODEL // tn

    def gspec(br):
        return pl.BlockSpec((tm, tn), lambda i, j, o=(OFF_GATE + br * D_MODEL) // tn: (i, o + j))

    return pl.pallas_call(
        _merge_kernel,
        grid=(t // tm, nj),
        in_specs=[pl.BlockSpec((tm, ML_V), lambda i, j: (i, 0)),
                  pl.BlockSpec((tm, RET_V), lambda i, j: (i, 0)),
                  pl.BlockSpec((tm, XA_W), lambda i, j: (i, 0)),
                  pl.BlockSpec((ML_V, tn), lambda i, j: (0, j)),
                  pl.BlockSpec((RET_V, tn), lambda i, j: (0, j)),
                  pl.BlockSpec((XA_W, tn), lambda i, j: (0, j)),
                  gspec(0), gspec(1), gspec(2)],
        out_specs=pl.BlockSpec((tm, tn), lambda i, j: (i, j)),
        out_shape=jax.ShapeDtypeStruct((t, D_MODEL), BF16),
        compiler_params=_params(("arbitrary", "arbitrary")),
        name="merge",
    )(ml_out, rt_out, xa_out, w_ml, w_rt, w_xa, p, p, p)


def _outproj_kernel(x_ref, m_ref, w_ref, g_ref, o_ref):
    y = x_ref[...] + _dot(m_ref[...], w_ref[...])
    o_ref[...] = _rms(y, g_ref[...])


def _outproj(x2, merged, w_out, final_g, tm):
    t = x2.shape[0]
    return pl.pallas_call(
        _outproj_kernel,
        grid=(t // tm,),
        in_specs=[pl.BlockSpec((tm, D_MODEL), lambda i: (i, 0)),
                  pl.BlockSpec((tm, D_MODEL), lambda i: (i, 0)),
                  pl.BlockSpec((D_MODEL, D_MODEL), lambda i: (0, 0)),
                  pl.BlockSpec((1, D_MODEL), lambda i: (0, 0))],
        out_specs=pl.BlockSpec((tm, D_MODEL), lambda i: (i, 0)),
        out_shape=jax.ShapeDtypeStruct((t, D_MODEL), F32),
        compiler_params=_params(("arbitrary",)),
        name="outproj",
    )(x2, merged, w_out, final_g)


def kernel(x, mem, positions, ln_g, mem_ln_g, w_in, b_in, conv_w, conv_b, ml_hnorm_g, ret_hnorm_g, w_mem_kv,
           w_br_ml, w_br_ret, w_br_xa, w_out, final_g):
    batch, seq, _ = x.shape
    assert seq % CHUNK == 0 and ln_g.shape[0] == 1
    t = batch * seq
    x2 = x.reshape(t, D_MODEL)

    wt, b0 = w_in[0].T, b_in[0]
    w_full = wt.astype(BF16)
    w_main = lax.dynamic_update_slice(w_full, w_full[OFF_IF + 2 * ML_HEADS:], (OFF_IF, 0))
    b_main = jnp.concatenate([b0[:OFF_IF], b0[OFF_IF + 2 * ML_HEADS:]])[None, :]
    wif_t = jnp.zeros((GATE_ROWS, D_MODEL), F32)
    wif_t = wif_t.at[0:ML_HEADS].set(wt[OFF_IF:OFF_IF + ML_HEADS])
    wif_t = wif_t.at[F_ROW:F_ROW + ML_HEADS].set(wt[OFF_IF + ML_HEADS:OFF_IF + 2 * ML_HEADS]).astype(BF16)
    bif_t = jnp.zeros((GATE_ROWS, 1), F32)
    bif_t = bif_t.at[0:ML_HEADS, 0].set(b0[OFF_IF:OFF_IF + ML_HEADS])
    bif_t = bif_t.at[F_ROW:F_ROW + ML_HEADS, 0].set(b0[OFF_IF + ML_HEADS:OFF_IF + 2 * ML_HEADS])

    memkv = _memkv(mem.reshape(batch * MEM_TOKENS, D_MODEL), mem_ln_g[0][None, :], w_mem_kv[0].astype(BF16))

    tm = min(1024, t)
    p, gates_t = _inproj(x2, ln_g[0][None, :], w_main, b_main, wif_t, bif_t, min(2048, t), 1024, 2)
    gates3 = gates_t.reshape(GATE_ROWS, t // CHUNK, CHUNK)

    half = RET_DQK // 2
    freqs = ROPE_BASE ** (-jnp.arange(half, dtype=F32) / half)
    freq2 = jnp.concatenate([freqs, freqs])[None, :]
    sign2 = jnp.concatenate([-jnp.ones((half,), F32), jnp.ones((half,), F32)])[None, :]
    cos2, sin2 = _rope_tables(positions.reshape(t, 1), freq2, sign2)
    log_gamma = jnp.asarray(np.log(1.0 - 2.0 ** (-5.0 - np.arange(RET_HEADS))), dtype=F32)

    ml_out, rt_out = _mixers(log_gamma, p, gates3, conv_w[0], conv_b[0][None, :], ml_hnorm_g[0][None, :],
                             cos2, sin2, ret_hnorm_g[0][None, :], batch, seq)
    xa_out = _xattn(p, memkv, batch, seq)

    merged = _merge(ml_out, rt_out, xa_out, w_br_ml[0].astype(BF16), w_br_ret[0].astype(BF16),
                    w_br_xa[0].astype(BF16), p, tm, 512)
    out = _outproj(x2, merged, w_out[0].astype(BF16), final_g[None, :], min(512, t))
    return out.reshape(batch, seq, D_MODEL)
```

```python
import functools

import numpy as np
import jax
import jax.numpy as jnp
from jax import lax
from jax.experimental import pallas as pl
from jax.experimental.pallas import tpu as pltpu

F32 = jnp.float32
BF16 = jnp.bfloat16

D_MODEL = 2048
MEM_TOKENS = 256
ML_HEADS = 6
ML_DQK = 128
ML_DV = 256
ML_CONV = 4
RET_HEADS = 6
RET_DQK = 128
RET_DV = 256
XA_HEADS = 4
XA_DH = 256
ROPE_BASE = 10000.0
EPS = 1e-6
N_BRANCH = 3

ML_QK = ML_HEADS * ML_DQK
ML_V = ML_HEADS * ML_DV
RET_QK = RET_HEADS * RET_DQK
RET_V = RET_HEADS * RET_DV
XA_W = XA_HEADS * XA_DH

OFF_ML_Q = 0
OFF_ML_K = OFF_ML_Q + ML_QK
OFF_ML_V = OFF_ML_K + ML_QK
OFF_ML_O = OFF_ML_V + ML_V
OFF_ML_Z = OFF_ML_O + ML_V
OFF_IF = OFF_ML_Z + ML_V
OFF_RT_Q = OFF_ML_Z + ML_V
OFF_RT_K = OFF_RT_Q + RET_QK
OFF_RT_V = OFF_RT_K + RET_QK
OFF_RT_Z = OFF_RT_V + RET_V
OFF_XA_Q = OFF_RT_Z + RET_V
OFF_XA_Z = OFF_XA_Q + XA_W
OFF_GATE = OFF_XA_Z + XA_W
N_MAIN = OFF_GATE + N_BRANCH * D_MODEL

CHUNK = 256
GATE_ROWS = 16
F_ROW = 8
CONV_BLOCK = 128

LANES = 128
VMEM_LIMIT = 56 * 1024 * 1024


def _params(sem):
    return pltpu.CompilerParams(dimension_semantics=sem, vmem_limit_bytes=VMEM_LIMIT)


def _sigmoid(x):
    return 0.5 * jnp.tanh(0.5 * x) + 0.5


def _rms(xf, g):
    return xf * lax.rsqrt(jnp.mean(xf * xf, axis=-1, keepdims=True) + EPS) * g


def _head_norm(t, g):
    mu = jnp.mean(t, axis=-1, keepdims=True)
    c = t - mu
    var = jnp.mean(c * c, axis=-1, keepdims=True)
    return c * lax.rsqrt(var + EPS) * g


def _dot(a, b):
    return jnp.dot(a, b, preferred_element_type=F32)


def _memkv_kernel(m_ref, g_ref, w_ref, o_ref):
    h = _rms(m_ref[...], g_ref[...])
    o_ref[...] = _dot(h.astype(BF16), w_ref[...]).astype(BF16)


def _memkv(mem2, g, w):
    rows = mem2.shape[0]
    tm = min(512, rows)
    return pl.pallas_call(
        _memkv_kernel,
        grid=(rows // tm,),
        in_specs=[pl.BlockSpec((tm, D_MODEL), lambda i: (i, 0)),
                  pl.BlockSpec((1, D_MODEL), lambda i: (0, 0)),
                  pl.BlockSpec((D_MODEL, 2 * XA_W), lambda i: (0, 0))],
        out_specs=pl.BlockSpec((tm, 2 * XA_W), lambda i: (i, 0)),
        out_shape=jax.ShapeDtypeStruct((rows, 2 * XA_W), BF16),
        compiler_params=_params(("arbitrary",)),
        name="memkv",
    )(mem2, g, w)


def _inproj_kernel(x_ref, g_ref, w_ref, b_ref, wif_ref, bif_ref, p_ref, gate_ref, hb_ref, *, nsplit, tail):
    j = pl.program_id(1)
    last = pl.num_programs(1) - 1
    tx = x_ref.shape[0]
    tn = w_ref.shape[0]

    def project(width):
        acc = lax.dot_general(hb_ref[...], w_ref[pl.ds(0, width), :], (((1,), (1,)), ((), ())),
                              preferred_element_type=F32)
        p_ref[:, pl.ds(0, width)] = (acc + b_ref[:, pl.ds(0, width)]).astype(BF16)

    @pl.when(j < nsplit)
    def _():
        hb = _rms(x_ref[...], g_ref[...]).astype(BF16)
        r0 = pl.multiple_of(j * tx, tx)
        hb_ref[pl.ds(r0, tx), :] = hb
        gt = lax.dot_general(wif_ref[...], hb, (((1,), (1,)), ((), ())), preferred_element_type=F32)
        gate_ref[:, pl.ds(r0, tx)] = gt + bif_ref[...]

    if tail == tn:
        pl.when(j >= nsplit)(lambda: project(tn))
    else:
        pl.when(jnp.logical_and(j >= nsplit, j < last))(lambda: project(tn))
        pl.when(j == last)(lambda: project(tail))


def _inproj(x2, g, w_main, b_main, wif_t, bif_t, tm, tn, nsplit):
    t = x2.shape[0]
    tx = tm // nsplit
    ncol = pl.cdiv(N_MAIN, tn)
    tail = N_MAIN - (ncol - 1) * tn

    def col(j):
        return jnp.maximum(j - nsplit, 0)

    return pl.pallas_call(
        functools.partial(_inproj_kernel, nsplit=nsplit, tail=tail),
        grid=(t // tm, nsplit + ncol),
        in_specs=[pl.BlockSpec((tx, D_MODEL), lambda i, j: (i * nsplit + jnp.minimum(j, nsplit - 1), 0)),
                  pl.BlockSpec((1, D_MODEL), lambda i, j: (0, 0)),
                  pl.BlockSpec((tn, D_MODEL), lambda i, j: (col(j), 0)),
                  pl.BlockSpec((1, tn), lambda i, j: (0, col(j))),
                  pl.BlockSpec((GATE_ROWS, D_MODEL), lambda i, j: (0, 0)),
                  pl.BlockSpec((GATE_ROWS, 1), lambda i, j: (0, 0))],
        out_specs=[pl.BlockSpec((tm, tn), lambda i, j: (i, col(j))),
                   pl.BlockSpec((GATE_ROWS, tm), lambda i, j: (0, i))],
        out_shape=[jax.ShapeDtypeStruct((t, N_MAIN), BF16),
                   jax.ShapeDtypeStruct((GATE_ROWS, t), F32)],
        scratch_shapes=[pltpu.VMEM((tm, D_MODEL), BF16)],
        compiler_params=_params(("arbitrary", "arbitrary")),
        name="inproj",
    )(x2, g, w_main, b_main, wif_t, bif_t)


def _rope_kernel(pos_ref, freq_ref, sign_ref, cos_ref, sin_ref):
    ang = pos_ref[...].astype(F32) * freq_ref[...]
    cos_ref[...] = jnp.cos(ang)
    sin_ref[...] = jnp.sin(ang) * sign_ref[...]


def _rope_tables(pos_col, freq2, sign2):
    t = pos_col.shape[0]
    tm = min(2048, t)
    return pl.pallas_call(
        _rope_kernel,
        grid=(t // tm,),
        in_specs=[pl.BlockSpec((tm, 1), lambda i: (i, 0)),
                  pl.BlockSpec((1, RET_DQK), lambda i: (0, 0)),
                  pl.BlockSpec((1, RET_DQK), lambda i: (0, 0))],
        out_specs=[pl.BlockSpec((tm, RET_DQK), lambda i: (i, 0)),
                   pl.BlockSpec((tm, RET_DQK), lambda i: (i, 0))],
        out_shape=[jax.ShapeDtypeStruct((t, RET_DQK), F32),
                   jax.ShapeDtypeStruct((t, RET_DQK), F32)],
        compiler_params=_params(("arbitrary",)),
        name="rope",
    )(pos_col, freq2, sign2)


def _split3(x):
    hi = x.astype(BF16)
    r1 = x - hi.astype(F32)
    mid = r1.astype(BF16)
    lo = (r1 - mid.astype(F32)).astype(BF16)
    return hi, mid, lo


def _conv_shift_matrix():
    s = np.zeros(((ML_CONV - 1) * CONV_BLOCK, 2 * CONV_BLOCK), np.float32)
    t = np.arange(CONV_BLOCK)
    for j in range(ML_CONV - 1):
        s[j * CONV_BLOCK + t, CONV_BLOCK + t - (ML_CONV - 1) + j] = 1.0
    return s


def _mlstm_setup(qk_ref, v_ref, o_ref, z_ref, gi_ref, gf_ref, cwq_ref, cwk_ref, cbq_ref, cbk_ref,
                 hg_ref, shift_ref, out_ref, pad_ref, qc_ref, kc_ref, u_ref, mr_ref, a_ref, w_ref, mp_ref, mn_ref,
                 dec_ref, caug_ref, *, seq):
    L = CHUNK
    nchunk = seq // L

    pad_ref[pl.ds(0, CONV_BLOCK), :] = jnp.zeros((CONV_BLOCK, 2 * ML_DQK), BF16)
    pad_ref[pl.ds(CONV_BLOCK, seq), :] = qk_ref[...]
    cw = jnp.concatenate([cwq_ref[...], cwk_ref[...]], axis=-1)
    cb = jnp.concatenate([cbq_ref[...], cbk_ref[...]], axis=-1)
    shift = shift_ref[...]

    def conv_block(i, carry):
        r = pl.multiple_of(i * CONV_BLOCK, CONV_BLOCK)
        taps = _dot(shift, pad_ref[pl.ds(r, 2 * CONV_BLOCK), :])
        y = cb + pad_ref[pl.ds(r + CONV_BLOCK, CONV_BLOCK), :].astype(F32) * cw[ML_CONV - 1:ML_CONV, :]
        for j in range(ML_CONV - 1):
            y = y + taps[j * CONV_BLOCK:(j + 1) * CONV_BLOCK, :] * cw[j:j + 1, :]
        yh = 0.5 * y
        y = yh * (1.0 + jnp.tanh(yh))
        qc_ref[pl.ds(r, CONV_BLOCK), :] = y[:, :ML_DQK] * (ML_DQK ** -0.5)
        kc_ref[pl.ds(r, CONV_BLOCK), :] = y[:, ML_DQK:]
        return carry

    lax.fori_loop(0, seq // CONV_BLOCK, conv_block, 0, unroll=8)

    f = gf_ref[0]
    lf = jnp.minimum(f, 0.0) - jnp.log1p(jnp.exp(-jnp.abs(f)))
    row = lax.broadcasted_iota(jnp.int32, (L, L), 0)
    col = lax.broadcasted_iota(jnp.int32, (L, L), 1)
    triu = (row <= col).astype(BF16)
    hi, mid, lo = _split3(lf)
    bb = _dot(hi, triu) + _dot(mid, triu) + _dot(lo, triu)
    causal = row >= col
    diag = row == col

    gi = gi_ref[0]
    g_col = bb[:, L - 1:L]
    log_w = g_col - bb + gi
    w_max = jnp.max(log_w, axis=-1, keepdims=True)
    m = jnp.zeros((1, 1), F32)
    for c in range(nchunk):
        mp_ref[c:c + 1, :] = m
        m = jnp.maximum(g_col[c:c + 1, :] + m, w_max[c:c + 1, :])
        mn_ref[c:c + 1, :] = m
    m_prev_all = mp_ref[...]
    m_new_all = mn_ref[...]
    a_all = gi - bb
    a_ref[...] = a_all
    w_ref[...] = jnp.exp(log_w - m_new_all)
    dec_ref[...] = jnp.exp(g_col + m_prev_all - m_new_all)
    lane = lax.broadcasted_iota(jnp.int32, a_all.shape, 1)
    run = a_all
    shift_by = 1
    while shift_by < L:
        run = jnp.maximum(run, jnp.where(lane >= shift_by, pltpu.roll(run, shift_by, 1), -jnp.inf))
        shift_by *= 2
    u_all = jnp.maximum(m_prev_all, run)
    u_ref[...] = u_all
    mr_ref[...] = bb + u_all
    eye3 = jnp.tile(diag.astype(BF16), (1, 3))

    caug_ref[...] = jnp.zeros_like(caug_ref)
    hg_half = 0.5 * hg_ref[...]
    ones_blk = jnp.ones((L, LANES), BF16)

    def chunk(c):
        r0 = pl.multiple_of(c * L, L)
        rows = pl.ds(r0, L)
        one = pl.ds(c, 1)
        a_row = a_ref[one, :]
        w_row = w_ref[one, :]
        m_prev = mp_ref[one, :]
        decay = dec_ref[one, :]
        u_pieces = jnp.concatenate(_split3(u_ref[one, :]), axis=1)
        m_pieces = jnp.concatenate(_split3(mr_ref[one, :]), axis=1)
        rows_t = jnp.concatenate([jnp.broadcast_to(u_pieces, (LANES, 3 * L)),
                                  jnp.broadcast_to(m_pieces, (LANES, 3 * L))], axis=0)
        um = lax.dot_general(eye3, rows_t, (((1,), (1,)), ((), ())), preferred_element_type=F32)
        u_rep = um[:, :LANES]
        d = jnp.exp(jnp.where(causal, a_row, -jnp.inf) - jnp.tile(u_rep, (1, L // LANES)))
        inter = jnp.exp(m_prev - u_rep)
        qf = qc_ref[rows, :]
        kf = kc_ref[rows, :]
        qb = qf.astype(BF16)
        kt = kf.T
        vb = v_ref[rows, :]
        s = _dot(qb, kt.astype(BF16)) * d
        caug = caug_ref[...]
        vaug = jnp.concatenate([vb, ones_blk], axis=1)
        na = _dot(s.astype(BF16), vaug) + jnp.tile(inter, (1, ML_DV // LANES + 1)) * _dot(qb, caug.astype(BF16))
        den = jnp.maximum(jnp.abs(na[:, ML_DV:]), jnp.exp(-um[:, LANES:]))

        caug_ref[...] = decay * caug + _dot((kt * w_row).astype(BF16), vaug)

        num = na[:, :ML_DV]
        cen = num - jnp.mean(num, axis=-1, keepdims=True)
        var = jnp.mean(cen * cen, axis=-1, keepdims=True)
        y = cen * jnp.tile(lax.rsqrt(var + EPS * (den * den)), (1, ML_DV // LANES)) * hg_half
        oh = 0.5 * o_ref[rows, :].astype(F32)
        zh = 0.5 * z_ref[rows, :].astype(F32)
        y = y * (1.0 + jnp.tanh(oh)) * (zh * (1.0 + jnp.tanh(zh)))
        out_ref[rows, :] = y.astype(BF16)

    return chunk


def _ret_setup(lg_ref, qk_ref, v_ref, z_ref, cos_ref, sin_ref, hg_ref, out_ref, r_ref):
    L = CHUNK
    lg = lg_ref[pl.program_id(1)]
    row = lax.broadcasted_iota(jnp.int32, (L, L), 0)
    col = lax.broadcasted_iota(jnp.int32, (L, L), 1)
    intra = jnp.where(row >= col, jnp.exp((row - col).astype(F32) * lg), 0.0)
    pos_col = lax.broadcasted_iota(jnp.int32, (L, 1), 0).astype(F32)
    pos_row = lax.broadcasted_iota(jnp.int32, (1, L), 1).astype(F32)
    q_decay = jnp.exp((pos_col + 1.0) * lg)
    k_decay = jnp.exp((L - 1.0 - pos_row) * lg)
    chunk_decay = jnp.exp(jnp.full((1, 1), float(L), F32) * lg)
    r_ref[...] = jnp.zeros_like(r_ref)
    half = RET_DQK // 2
    hg = hg_ref[...]

    def chunk(c):
        r0 = pl.multiple_of(c * L, L)
        rows = pl.ds(r0, L)
        cs = cos_ref[rows, :]
        sn = sin_ref[rows, :]
        qf = qk_ref[rows, pl.ds(0, RET_DQK)].astype(F32)
        kf = qk_ref[rows, pl.ds(RET_DQK, RET_DQK)].astype(F32)
        qr = qf * cs + pltpu.roll(qf, half, 1) * sn
        kr = (kf * cs + pltpu.roll(kf, half, 1) * sn) * (RET_DQK ** -0.5)
        qb = qr.astype(BF16)
        kt = kr.T
        vb = v_ref[rows, :]
        s = _dot(qb, kt.astype(BF16)) * intra
        r_state = r_ref[...]
        o = _dot(s.astype(BF16), vb) + q_decay * _dot(qb, r_state.astype(BF16))
        r_ref[...] = chunk_decay * r_state + _dot((kt * k_decay).astype(BF16), vb)
        y = _head_norm(o, hg)
        zh = 0.5 * z_ref[rows, :].astype(F32)
        out_ref[rows, :] = (y * (zh * (1.0 + jnp.tanh(zh)))).astype(BF16)

    return chunk


N_ML_IN = 12
N_RT_IN = 6
N_ML_SCRATCH = 11
CHUNK_UNROLL = 4


def _mixers_kernel(lg_ref, *refs, seq):
    ml_in = refs[:N_ML_IN]
    rt_in = refs[N_ML_IN:N_ML_IN + N_RT_IN]
    ml_out, rt_out = refs[N_ML_IN + N_RT_IN:N_ML_IN + N_RT_IN + 2]
    scratch = refs[N_ML_IN + N_RT_IN + 2:]
    nchunk = seq // CHUNK
    ml_chunk = _mlstm_setup(*ml_in, ml_out, *scratch[:N_ML_SCRATCH], seq=seq)
    lax.fori_loop(0, nchunk, lambda c, carry: (ml_chunk(c), carry)[1], 0, unroll=CHUNK_UNROLL)
    rt_chunk = _ret_setup(lg_ref, *rt_in, rt_out, scratch[N_ML_SCRATCH])
    lax.fori_loop(0, nchunk, lambda c, carry: (rt_chunk(c), carry)[1], 0, unroll=CHUNK_UNROLL)


def _mixers(log_gamma, p, gates3, conv_w, conv_b, ml_g, cos2, sin2, rt_g, batch, seq):
    t = batch * seq
    nchunk = seq // CHUNK
    qb, vb = ML_DQK, ML_DV
    assert (ML_HEADS, ML_DQK, ML_DV) == (RET_HEADS, RET_DQK, RET_DV)

    def pspec(width, off):
        return pl.BlockSpec((seq, width), lambda b, h, lg, off=off // width: (b, off + h))

    def gspec(row0):
        return pl.BlockSpec((1, nchunk, CHUNK), lambda b, h, lg, row0=row0: (row0 + h, b, 0))

    ml_specs = [pspec(2 * qb, OFF_ML_Q), pspec(vb, OFF_ML_V), pspec(vb, OFF_ML_O),
                pspec(vb, OFF_ML_Z), gspec(0), gspec(F_ROW),
                pl.BlockSpec((ML_CONV, qb), lambda b, h, lg: (0, h)),
                pl.BlockSpec((ML_CONV, qb), lambda b, h, lg: (0, ML_HEADS + h)),
                pl.BlockSpec((1, qb), lambda b, h, lg: (0, h)),
                pl.BlockSpec((1, qb), lambda b, h, lg: (0, ML_HEADS + h)),
                pl.BlockSpec((1, vb), lambda b, h, lg: (0, h)),
                pl.BlockSpec(((ML_CONV - 1) * CONV_BLOCK, 2 * CONV_BLOCK), lambda b, h, lg: (0, 0))]
    rt_specs = [pspec(2 * qb, OFF_RT_Q), pspec(vb, OFF_RT_V), pspec(vb, OFF_RT_Z),
                pl.BlockSpec((seq, qb), lambda b, h, lg: (b, 0)),
                pl.BlockSpec((seq, qb), lambda b, h, lg: (b, 0)),
                pl.BlockSpec((1, vb), lambda b, h, lg: (0, h))]
    assert len(ml_specs) == N_ML_IN and len(rt_specs) == N_RT_IN
    ml_scratch = [pltpu.VMEM((seq + CONV_BLOCK, 2 * qb), BF16),
                  pltpu.VMEM((seq, qb), F32),
                  pltpu.VMEM((seq, qb), F32),
                  pltpu.VMEM((nchunk, CHUNK), F32),
                  pltpu.VMEM((nchunk, CHUNK), F32),
                  pltpu.VMEM((nchunk, CHUNK), F32),
                  pltpu.VMEM((nchunk, CHUNK), F32),
                  pltpu.VMEM((nchunk, 1), F32),
                  pltpu.VMEM((nchunk, 1), F32),
                  pltpu.VMEM((nchunk, 1), F32),
                  pltpu.VMEM((qb, vb + LANES), F32)]
    assert len(ml_scratch) == N_ML_SCRATCH
    out_spec = pl.BlockSpec((seq, vb), lambda b, h, lg: (b, h))
    grid_spec = pltpu.PrefetchScalarGridSpec(
        num_scalar_prefetch=1,
        grid=(batch, ML_HEADS),
        in_specs=ml_specs + rt_specs,
        out_specs=[out_spec, out_spec],
        scratch_shapes=ml_scratch + [pltpu.VMEM((qb, vb), F32)],
    )
    return pl.pallas_call(
        functools.partial(_mixers_kernel, seq=seq),
        grid_spec=grid_spec,
        out_shape=[jax.ShapeDtypeStruct((t, ML_V), BF16), jax.ShapeDtypeStruct((t, RET_V), BF16)],
        compiler_params=_params(("arbitrary", "arbitrary")),
        name="mixers",
    )(log_gamma, p, p, p, p, gates3, gates3, conv_w, conv_w, conv_b, conv_b, ml_g,
      jnp.asarray(_conv_shift_matrix(), BF16), p, p, p, cos2, sin2, rt_g)


def _xattn_kernel(q_ref, z_ref, mk_ref, mv_ref, out_ref, *, blk):
    mk = mk_ref[...]
    mv = mv_ref[...]

    def block(i, carry):
        rows = pl.ds(pl.multiple_of(i * blk, blk), blk)
        sc = lax.dot_general(q_ref[rows, :], mk, (((1,), (1,)), ((), ())), preferred_element_type=F32)
        sc = sc * (XA_DH ** -0.5)
        e = jnp.exp(sc - jnp.max(sc, axis=-1, keepdims=True))
        o = _dot(e.astype(BF16), mv) * (1.0 / jnp.sum(e, axis=-1, keepdims=True))
        zh = 0.5 * z_ref[rows, :].astype(F32)
        out_ref[rows, :] = (o * (zh * (1.0 + jnp.tanh(zh)))).astype(BF16)
        return carry

    lax.fori_loop(0, q_ref.shape[0] // blk, block, 0, unroll=2)


def _xattn(p, memkv, batch, seq):
    t = batch * seq
    ts = seq
    ns = seq // ts
    w = XA_DH
    return pl.pallas_call(
        functools.partial(_xattn_kernel, blk=min(1024, ts)),
        grid=(batch, XA_HEADS, ns),
        in_specs=[pl.BlockSpec((ts, w), lambda b, h, s: (b * ns + s, OFF_XA_Q // w + h)),
                  pl.BlockSpec((ts, w), lambda b, h, s: (b * ns + s, OFF_XA_Z // w + h)),
                  pl.BlockSpec((MEM_TOKENS, w), lambda b, h, s: (b, h)),
                  pl.BlockSpec((MEM_TOKENS, w), lambda b, h, s: (b, XA_HEADS + h))],
        out_specs=pl.BlockSpec((ts, w), lambda b, h, s: (b * ns + s, h)),
        out_shape=jax.ShapeDtypeStruct((t, XA_W), BF16),
        compiler_params=_params(("arbitrary", "arbitrary", "arbitrary")),
        name="xattn",
    )(p, p, memkv, memkv)


def _merge_kernel(ml_ref, rt_ref, xa_ref, wml_ref, wrt_ref, wxa_ref, g0_ref, g1_ref, g2_ref, o_ref):
    acc = _sigmoid(g0_ref[...].astype(F32)) * _dot(ml_ref[...], wml_ref[...])
    acc = acc + _sigmoid(g1_ref[...].astype(F32)) * _dot(rt_ref[...], wrt_ref[...])
    acc = acc + _sigmoid(g2_ref[...].astype(F32)) * _dot(xa_ref[...], wxa_ref[...])
    o_ref[...] = acc.astype(BF16)


def _merge(ml_out, rt_out, xa_out, w_ml, w_rt, w_xa, p, tm, tn):
    t = ml_out.shape[0]
    nj = D_MODEL // tn

    def gspec(br):
        return pl.BlockSpec((tm, tn), lambda i, j, o=(OFF_GATE + br * D_MODEL) // tn: (i, o + j))

    return pl.pallas_call(
        _merge_kernel,
        grid=(t // tm, nj),
        in_specs=[pl.BlockSpec((tm, ML_V), lambda i, j: (i, 0)),
                  pl.BlockSpec((tm, RET_V), lambda i, j: (i, 0)),
                  pl.BlockSpec((tm, XA_W), lambda i, j: (i, 0)),
                  pl.BlockSpec((ML_V, tn), lambda i, j: (0, j)),
                  pl.BlockSpec((RET_V, tn), lambda i, j: (0, j)),
                  pl.BlockSpec((XA_W, tn), lambda i, j: (0, j)),
                  gspec(0), gspec(1), gspec(2)],
        out_specs=pl.BlockSpec((tm, tn), lambda i, j: (i, j)),
        out_shape=jax.ShapeDtypeStruct((t, D_MODEL), BF16),
        compiler_params=_params(("arbitrary", "arbitrary")),
        name="merge",
    )(ml_out, rt_out, xa_out, w_ml, w_rt, w_xa, p, p, p)


def _outproj_kernel(x_ref, m_ref, w_ref, g_ref, o_ref):
    y = x_ref[...] + _dot(m_ref[...], w_ref[...])
    o_ref[...] = _rms(y, g_ref[...])


def _outproj(x2, merged, w_out, final_g, tm):
    t = x2.shape[0]
    return pl.pallas_call(
        _outproj_kernel,
        grid=(t // tm,),
        in_specs=[pl.BlockSpec((tm, D_MODEL), lambda i: (i, 0)),
                  pl.BlockSpec((tm, D_MODEL), lambda i: (i, 0)),
                  pl.BlockSpec((D_MODEL, D_MODEL), lambda i: (0, 0)),
                  pl.BlockSpec((1, D_MODEL), lambda i: (0, 0))],
        out_specs=pl.BlockSpec((tm, D_MODEL), lambda i: (i, 0)),
        out_shape=jax.ShapeDtypeStruct((t, D_MODEL), F32),
        compiler_params=_params(("arbitrary",)),
        name="outproj",
    )(x2, merged, w_out, final_g)


def _p_segments():
    segs = []

    def interleave(q0, k0, heads, dqk):
        for h in range(heads):
            segs.append((q0 + h * dqk, q0 + (h + 1) * dqk))
            segs.append((k0 + h * dqk, k0 + (h + 1) * dqk))

    interleave(0, ML_QK, ML_HEADS, ML_DQK)
    segs.append((2 * ML_QK, OFF_IF))
    rt0 = OFF_IF + 2 * ML_HEADS
    interleave(rt0, rt0 + RET_QK, RET_HEADS, RET_DQK)
    segs.append((rt0 + 2 * RET_QK, N_MAIN + 2 * ML_HEADS))
    return segs


def kernel(x, mem, positions, ln_g, mem_ln_g, w_in, b_in, conv_w, conv_b, ml_hnorm_g, ret_hnorm_g, w_mem_kv,
           w_br_ml, w_br_ret, w_br_xa, w_out, final_g):
    batch, seq, _ = x.shape
    assert seq % CHUNK == 0 and ln_g.shape[0] == 1
    t = batch * seq
    x2 = x.reshape(t, D_MODEL)

    wt, b0 = w_in[0].T, b_in[0]
    w_full = wt.astype(BF16)
    w_main = jnp.concatenate([w_full[a:b] for a, b in _p_segments()], axis=0)
    b_main = jnp.concatenate([b0[a:b] for a, b in _p_segments()])[None, :]
    wif_t = jnp.zeros((GATE_ROWS, D_MODEL), F32)
    wif_t = wif_t.at[0:ML_HEADS].set(wt[OFF_IF:OFF_IF + ML_HEADS])
    wif_t = wif_t.at[F_ROW:F_ROW + ML_HEADS].set(wt[OFF_IF + ML_HEADS:OFF_IF + 2 * ML_HEADS]).astype(BF16)
    bif_t = jnp.zeros((GATE_ROWS, 1), F32)
    bif_t = bif_t.at[0:ML_HEADS, 0].set(b0[OFF_IF:OFF_IF + ML_HEADS])
    bif_t = bif_t.at[F_ROW:F_ROW + ML_HEADS, 0].set(b0[OFF_IF + ML_HEADS:OFF_IF + 2 * ML_HEADS])

    memkv = _memkv(mem.reshape(batch * MEM_TOKENS, D_MODEL), mem_ln_g[0][None, :], w_mem_kv[0].astype(BF16))

    tm = min(1024, t)
    p, gates_t = _inproj(x2, ln_g[0][None, :], w_main, b_main, wif_t, bif_t, min(2048, t), 1024, 2)
    gates3 = gates_t.reshape(GATE_ROWS, t // CHUNK, CHUNK)

    half = RET_DQK // 2
    freqs = ROPE_BASE ** (-jnp.arange(half, dtype=F32) / half)
    freq2 = jnp.concatenate([freqs, freqs])[None, :]
    sign2 = jnp.concatenate([-jnp.ones((half,), F32), jnp.ones((half,), F32)])[None, :]
    cos2, sin2 = _rope_tables(positions.reshape(t, 1), freq2, sign2)
    log_gamma = jnp.asarray(np.log(1.0 - 2.0 ** (-5.0 - np.arange(RET_HEADS))), dtype=F32)

    ml_out, rt_out = _mixers(log_gamma, p, gates3, conv_w[0], conv_b[0][None, :], ml_hnorm_g[0][None, :],
                             cos2, sin2, ret_hnorm_g[0][None, :], batch, seq)
    xa_out = _xattn(p, memkv, batch, seq)

    merged = _merge(ml_out, rt_out, xa_out, w_br_ml[0].astype(BF16), w_br_ret[0].astype(BF16),
                    w_br_xa[0].astype(BF16), p, tm, 512)
    out = _outproj(x2, merged, w_out[0].astype(BF16), final_g[None, :], min(512, t))
    return out.reshape(batch, seq, D_MODEL)
```

```python
import functools

import numpy as np
import jax
import jax.numpy as jnp
from jax import lax
from jax.experimental import pallas as pl
from jax.experimental.pallas import tpu as pltpu

F32 = jnp.float32
BF16 = jnp.bfloat16

D_MODEL = 2048
MEM_TOKENS = 256
ML_HEADS = 6
ML_DQK = 128
ML_DV = 256
ML_CONV = 4
RET_HEADS = 6
RET_DQK = 128
RET_DV = 256
XA_HEADS = 4
XA_DH = 256
ROPE_BASE = 10000.0
EPS = 1e-6
N_BRANCH = 3

ML_QK = ML_HEADS * ML_DQK
ML_V = ML_HEADS * ML_DV
RET_QK = RET_HEADS * RET_DQK
RET_V = RET_HEADS * RET_DV
XA_W = XA_HEADS * XA_DH

OFF_ML_Q = 0
OFF_ML_K = OFF_ML_Q + ML_QK
OFF_ML_V = OFF_ML_K + ML_QK
OFF_ML_O = OFF_ML_V + ML_V
OFF_ML_Z = OFF_ML_O + ML_V
OFF_IF = OFF_ML_Z + ML_V
OFF_RT_Q = OFF_ML_Z + ML_V
OFF_RT_K = OFF_RT_Q + RET_QK
OFF_RT_V = OFF_RT_K + RET_QK
OFF_RT_Z = OFF_RT_V + RET_V
OFF_XA_Q = OFF_RT_Z + RET_V
OFF_XA_Z = OFF_XA_Q + XA_W
OFF_GATE = OFF_XA_Z + XA_W
N_MAIN = OFF_GATE + N_BRANCH * D_MODEL

CHUNK = 256
GATE_ROWS = 16
F_ROW = 8
CONV_BLOCK = 128

LANES = 128
VMEM_LIMIT = 56 * 1024 * 1024


def _params(sem):
    return pltpu.CompilerParams(dimension_semantics=sem, vmem_limit_bytes=VMEM_LIMIT)


def _sigmoid(x):
    return 0.5 * jnp.tanh(0.5 * x) + 0.5


def _rms(xf, g):
    return xf * lax.rsqrt(jnp.mean(xf * xf, axis=-1, keepdims=True) + EPS) * g


def _head_norm(t, g):
    mu = jnp.mean(t, axis=-1, keepdims=True)
    c = t - mu
    var = jnp.mean(c * c, axis=-1, keepdims=True)
    return c * lax.rsqrt(var + EPS) * g


def _dot(a, b):
    return jnp.dot(a, b, preferred_element_type=F32)


def _memkv_kernel(m_ref, g_ref, w_ref, o_ref):
    h = _rms(m_ref[...], g_ref[...])
    o_ref[...] = _dot(h.astype(BF16), w_ref[...]).astype(BF16)


def _memkv(mem2, g, w):
    rows = mem2.shape[0]
    tm = min(512, rows)
    return pl.pallas_call(
        _memkv_kernel,
        grid=(rows // tm,),
        in_specs=[pl.BlockSpec((tm, D_MODEL), lambda i: (i, 0)),
                  pl.BlockSpec((1, D_MODEL), lambda i: (0, 0)),
                  pl.BlockSpec((D_MODEL, 2 * XA_W), lambda i: (0, 0))],
        out_specs=pl.BlockSpec((tm, 2 * XA_W), lambda i: (i, 0)),
        out_shape=jax.ShapeDtypeStruct((rows, 2 * XA_W), BF16),
        compiler_params=_params(("arbitrary",)),
        name="memkv",
    )(mem2, g, w)


def _inproj_kernel(x_ref, g_ref, w_ref, b_ref, wif_ref, bif_ref, p_ref, gate_ref, hb_ref, *, nsplit, tail):
    j = pl.program_id(1)
    last = pl.num_programs(1) - 1
    tx = x_ref.shape[0]
    tn = w_ref.shape[0]

    def project(width):
        acc = lax.dot_general(hb_ref[...], w_ref[pl.ds(0, width), :], (((1,), (1,)), ((), ())),
                              preferred_element_type=F32)
        p_ref[:, pl.ds(0, width)] = (acc + b_ref[:, pl.ds(0, width)]).astype(BF16)

    @pl.when(j < nsplit)
    def _():
        hb = _rms(x_ref[...], g_ref[...]).astype(BF16)
        r0 = pl.multiple_of(j * tx, tx)
        hb_ref[pl.ds(r0, tx), :] = hb
        gt = lax.dot_general(wif_ref[...], hb, (((1,), (1,)), ((), ())), preferred_element_type=F32)
        gate_ref[:, pl.ds(r0, tx)] = gt + bif_ref[...]

    if tail == tn:
        pl.when(j >= nsplit)(lambda: project(tn))
    else:
        pl.when(jnp.logical_and(j >= nsplit, j < last))(lambda: project(tn))
        pl.when(j == last)(lambda: project(tail))


def _inproj(x2, g, w_main, b_main, wif_t, bif_t, tm, tn, nsplit):
    t = x2.shape[0]
    tx = tm // nsplit
    ncol = pl.cdiv(N_MAIN, tn)
    tail = N_MAIN - (ncol - 1) * tn

    def col(j):
        return jnp.maximum(j - nsplit, 0)

    return pl.pallas_call(
        functools.partial(_inproj_kernel, nsplit=nsplit, tail=tail),
        grid=(t // tm, nsplit + ncol),
        in_specs=[pl.BlockSpec((tx, D_MODEL), lambda i, j: (i * nsplit + jnp.minimum(j, nsplit - 1), 0)),
                  pl.BlockSpec((1, D_MODEL), lambda i, j: (0, 0)),
                  pl.BlockSpec((tn, D_MODEL), lambda i, j: (col(j), 0)),
                  pl.BlockSpec((1, tn), lambda i, j: (0, col(j))),
                  pl.BlockSpec((GATE_ROWS, D_MODEL), lambda i, j: (0, 0)),
                  pl.BlockSpec((GATE_ROWS, 1), lambda i, j: (0, 0))],
        out_specs=[pl.BlockSpec((tm, tn), lambda i, j: (i, col(j))),
                   pl.BlockSpec((GATE_ROWS, tm), lambda i, j: (0, i))],
        out_shape=[jax.ShapeDtypeStruct((t, N_MAIN), BF16),
                   jax.ShapeDtypeStruct((GATE_ROWS, t), F32)],
        scratch_shapes=[pltpu.VMEM((tm, D_MODEL), BF16)],
        compiler_params=_params(("arbitrary", "arbitrary")),
        name="inproj",
    )(x2, g, w_main, b_main, wif_t, bif_t)


def _rope_kernel(pos_ref, freq_ref, cos_ref, sin_ref):
    half = RET_DQK // 2
    low = lax.broadcasted_iota(jnp.int32, (1, RET_DQK), 1) < half
    pos = jnp.where(low, pos_ref[:, 0:1], pos_ref[:, 1:2]).astype(F32)
    ang = pos * freq_ref[...]
    c = jnp.cos(ang)
    s = jnp.sin(ang)
    c_other = pltpu.roll(c, half, 1)
    s_other = pltpu.roll(s, half, 1)
    cos_ref[0] = jnp.where(low, c, c_other)
    cos_ref[1] = jnp.where(low, c_other, c)
    sin_ref[0] = jnp.where(low, -s, s_other)
    sin_ref[1] = jnp.where(low, -s_other, s)


def _rope_tables(pos_pair, freq2):
    th = pos_pair.shape[0]
    tm = min(2048, th)
    out_spec = pl.BlockSpec((2, tm, RET_DQK), lambda i: (0, i, 0))
    out_shape = jax.ShapeDtypeStruct((2, th, RET_DQK), F32)
    cos2, sin2 = pl.pallas_call(
        _rope_kernel,
        grid=(th // tm,),
        in_specs=[pl.BlockSpec((tm, 2), lambda i: (i, 0)),
                  pl.BlockSpec((1, RET_DQK), lambda i: (0, 0))],
        out_specs=[out_spec, out_spec],
        out_shape=[out_shape, out_shape],
        compiler_params=_params(("arbitrary",)),
        name="rope",
    )(pos_pair, freq2)
    return cos2.reshape(2 * th, RET_DQK), sin2.reshape(2 * th, RET_DQK)


def _split3(x):
    hi = x.astype(BF16)
    r1 = x - hi.astype(F32)
    mid = r1.astype(BF16)
    lo = (r1 - mid.astype(F32)).astype(BF16)
    return hi, mid, lo


def _conv_shift_matrix():
    s = np.zeros(((ML_CONV - 1) * CONV_BLOCK, 2 * CONV_BLOCK), np.float32)
    t = np.arange(CONV_BLOCK)
    for j in range(ML_CONV - 1):
        s[j * CONV_BLOCK + t, CONV_BLOCK + t - (ML_CONV - 1) + j] = 1.0
    return s


def _mlstm_setup(q_ref, k_ref, v_ref, o_ref, z_ref, gi_ref, gf_ref, cwq_ref, cwk_ref, cbq_ref, cbk_ref,
                 hg_ref, shift_ref, out_ref, pad_ref, qc_ref, kc_ref, u_ref, mr_ref, a_ref, w_ref, mp_ref, mn_ref,
                 dec_ref, caug_ref, *, seq):
    L = CHUNK
    nchunk = seq // L

    pad_ref[pl.ds(0, CONV_BLOCK), :] = jnp.zeros((CONV_BLOCK, 2 * ML_DQK), BF16)
    pad_ref[pl.ds(CONV_BLOCK, seq), pl.ds(0, ML_DQK)] = q_ref[...]
    pad_ref[pl.ds(CONV_BLOCK, seq), pl.ds(ML_DQK, ML_DQK)] = k_ref[...]
    cw = jnp.concatenate([cwq_ref[...], cwk_ref[...]], axis=-1)
    cb = jnp.concatenate([cbq_ref[...], cbk_ref[...]], axis=-1)
    shift = shift_ref[...]

    def conv_block(i, carry):
        r = pl.multiple_of(i * CONV_BLOCK, CONV_BLOCK)
        taps = _dot(shift, pad_ref[pl.ds(r, 2 * CONV_BLOCK), :])
        y = cb + pad_ref[pl.ds(r + CONV_BLOCK, CONV_BLOCK), :].astype(F32) * cw[ML_CONV - 1:ML_CONV, :]
        for j in range(ML_CONV - 1):
            y = y + taps[j * CONV_BLOCK:(j + 1) * CONV_BLOCK, :] * cw[j:j + 1, :]
        yh = 0.5 * y
        y = yh * (1.0 + jnp.tanh(yh))
        qc_ref[pl.ds(r, CONV_BLOCK), :] = y[:, :ML_DQK] * (ML_DQK ** -0.5)
        kc_ref[pl.ds(r, CONV_BLOCK), :] = y[:, ML_DQK:]
        return carry

    lax.fori_loop(0, seq // CONV_BLOCK, conv_block, 0, unroll=8)

    f = gf_ref[0]
    lf = jnp.minimum(f, 0.0) - jnp.log1p(jnp.exp(-jnp.abs(f)))
    row = lax.broadcasted_iota(jnp.int32, (L, L), 0)
    col = lax.broadcasted_iota(jnp.int32, (L, L), 1)
    triu = (row <= col).astype(BF16)
    hi, mid, lo = _split3(lf)
    bb = _dot(hi, triu) + _dot(mid, triu) + _dot(lo, triu)
    causal = row >= col
    diag = row == col

    gi = gi_ref[0]
    g_col = bb[:, L - 1:L]
    log_w = g_col - bb + gi
    w_max = jnp.max(log_w, axis=-1, keepdims=True)
    m = jnp.zeros((1, 1), F32)
    for c in range(nchunk):
        mp_ref[c:c + 1, :] = m
        m = jnp.maximum(g_col[c:c + 1, :] + m, w_max[c:c + 1, :])
        mn_ref[c:c + 1, :] = m
    m_prev_all = mp_ref[...]
    m_new_all = mn_ref[...]
    a_all = gi - bb
    a_ref[...] = a_all
    w_ref[...] = jnp.exp(log_w - m_new_all)
    dec_ref[...] = jnp.exp(g_col + m_prev_all - m_new_all)
    lane = lax.broadcasted_iota(jnp.int32, a_all.shape, 1)
    run = a_all
    shift_by = 1
    while shift_by < L:
        run = jnp.maximum(run, jnp.where(lane >= shift_by, pltpu.roll(run, shift_by, 1), -jnp.inf))
        shift_by *= 2
    u_all = jnp.maximum(m_prev_all, run)
    u_ref[...] = u_all
    mr_ref[...] = bb + u_all
    eye3 = jnp.tile(diag.astype(BF16), (1, 3))

    caug_ref[...] = jnp.zeros_like(caug_ref)
    hg_half = 0.5 * hg_ref[...]
    ones_blk = jnp.ones((L, LANES), BF16)

    def chunk(c):
        r0 = pl.multiple_of(c * L, L)
        rows = pl.ds(r0, L)
        one = pl.ds(c, 1)
        a_row = a_ref[one, :]
        w_row = w_ref[one, :]
        m_prev = mp_ref[one, :]
        decay = dec_ref[one, :]
        u_pieces = jnp.concatenate(_split3(u_ref[one, :]), axis=1)
        m_pieces = jnp.concatenate(_split3(mr_ref[one, :]), axis=1)
        rows_t = jnp.concatenate([jnp.broadcast_to(u_pieces, (LANES, 3 * L)),
                                  jnp.broadcast_to(m_pieces, (LANES, 3 * L))], axis=0)
        um = lax.dot_general(eye3, rows_t, (((1,), (1,)), ((), ())), preferred_element_type=F32)
        u_rep = um[:, :LANES]
        d = jnp.exp(jnp.where(causal, a_row, -jnp.inf) - jnp.tile(u_rep, (1, L // LANES)))
        inter = jnp.exp(m_prev - u_rep)
        qf = qc_ref[rows, :]
        kf = kc_ref[rows, :]
        qb = qf.astype(BF16)
        kt = kf.T
        vb = v_ref[rows, :]
        s = _dot(qb, kt.astype(BF16)) * d
        caug = caug_ref[...]
        vaug = jnp.concatenate([vb, ones_blk], axis=1)
        na = _dot(s.astype(BF16), vaug) + jnp.tile(inter, (1, ML_DV // LANES + 1)) * _dot(qb, caug.astype(BF16))
        den = jnp.maximum(jnp.abs(na[:, ML_DV:]), jnp.exp(-um[:, LANES:]))

        caug_ref[...] = decay * caug + _dot((kt * w_row).astype(BF16), vaug)

        num = na[:, :ML_DV]
        cen = num - jnp.mean(num, axis=-1, keepdims=True)
        var = jnp.mean(cen * cen, axis=-1, keepdims=True)
        y = cen * jnp.tile(lax.rsqrt(var + EPS * (den * den)), (1, ML_DV // LANES)) * hg_half
        oh = 0.5 * o_ref[rows, :].astype(F32)
        zh = 0.5 * z_ref[rows, :].astype(F32)
        y = y * (1.0 + jnp.tanh(oh)) * (zh * (1.0 + jnp.tanh(zh)))
        out_ref[rows, :] = y.astype(BF16)

    return chunk


def _ret_setup(lg_ref, q_ref, k_ref, v_ref, z_ref, cos_ref, sin_ref, hg_ref, out_ref, r_ref):
    L = CHUNK
    lg = lg_ref[pl.program_id(1)]
    row = lax.broadcasted_iota(jnp.int32, (L, L), 0)
    col = lax.broadcasted_iota(jnp.int32, (L, L), 1)
    intra = jnp.where(row >= col, jnp.exp((row - col).astype(F32) * lg), 0.0)
    pos_col = lax.broadcasted_iota(jnp.int32, (L, 1), 0).astype(F32)
    pos_row = lax.broadcasted_iota(jnp.int32, (1, L), 1).astype(F32)
    q_decay = jnp.exp((pos_col + 1.0) * lg)
    k_decay = jnp.exp((L - 1.0 - pos_row) * lg)
    chunk_decay = jnp.exp(jnp.full((1, 1), float(L), F32) * lg)
    r_ref[...] = jnp.zeros_like(r_ref)
    half = RET_DQK // 2
    hg = hg_ref[...]

    def chunk(c):
        r0 = pl.multiple_of(c * L, L)
        rows = pl.ds(r0, L)
        cs = cos_ref[rows, :]
        sn = sin_ref[rows, :]
        qf = q_ref[rows, :].astype(F32)
        kf = k_ref[rows, :].astype(F32)
        qr = qf * cs + pltpu.roll(qf, half, 1) * sn
        kr = (kf * cs + pltpu.roll(kf, half, 1) * sn) * (RET_DQK ** -0.5)
        qb = qr.astype(BF16)
        kt = kr.T
        vb = v_ref[rows, :]
        s = _dot(qb, kt.astype(BF16)) * intra
        r_state = r_ref[...]
        o = _dot(s.astype(BF16), vb) + q_decay * _dot(qb, r_state.astype(BF16))
        r_ref[...] = chunk_decay * r_state + _dot((kt * k_decay).astype(BF16), vb)
        y = _head_norm(o, hg)
        zh = 0.5 * z_ref[rows, :].astype(F32)
        out_ref[rows, :] = (y * (zh * (1.0 + jnp.tanh(zh)))).astype(BF16)

    return chunk


N_ML_IN = 13
N_RT_IN = 7
N_ML_SCRATCH = 11
CHUNK_UNROLL = 4


def _mixers_kernel(lg_ref, *refs, seq):
    ml_in = refs[:N_ML_IN]
    rt_in = refs[N_ML_IN:N_ML_IN + N_RT_IN]
    ml_out, rt_out = refs[N_ML_IN + N_RT_IN:N_ML_IN + N_RT_IN + 2]
    scratch = refs[N_ML_IN + N_RT_IN + 2:]
    nchunk = seq // CHUNK
    ml_chunk = _mlstm_setup(*ml_in, ml_out, *scratch[:N_ML_SCRATCH], seq=seq)
    lax.fori_loop(0, nchunk, lambda c, carry: (ml_chunk(c), carry)[1], 0, unroll=CHUNK_UNROLL)
    rt_chunk = _ret_setup(lg_ref, *rt_in, rt_out, scratch[N_ML_SCRATCH])
    lax.fori_loop(0, nchunk, lambda c, carry: (rt_chunk(c), carry)[1], 0, unroll=CHUNK_UNROLL)


def _mixers(log_gamma, p, gates3, conv_w, conv_b, ml_g, cos2, sin2, rt_g, batch, seq):
    t = batch * seq
    nchunk = seq // CHUNK
    qb, vb = ML_DQK, ML_DV
    assert (ML_HEADS, ML_DQK, ML_DV) == (RET_HEADS, RET_DQK, RET_DV)

    def pspec(width, off):
        return pl.BlockSpec((seq, width), lambda b, h, lg, off=off // width: (b, off + h))

    def gspec(row0):
        return pl.BlockSpec((1, nchunk, CHUNK), lambda b, h, lg, row0=row0: (row0 + h, b, 0))

    ml_specs = [pspec(qb, OFF_ML_Q), pspec(qb, OFF_ML_K), pspec(vb, OFF_ML_V), pspec(vb, OFF_ML_O),
                pspec(vb, OFF_ML_Z), gspec(0), gspec(F_ROW),
                pl.BlockSpec((ML_CONV, qb), lambda b, h, lg: (0, h)),
                pl.BlockSpec((ML_CONV, qb), lambda b, h, lg: (0, ML_HEADS + h)),
                pl.BlockSpec((1, qb), lambda b, h, lg: (0, h)),
                pl.BlockSpec((1, qb), lambda b, h, lg: (0, ML_HEADS + h)),
                pl.BlockSpec((1, vb), lambda b, h, lg: (0, h)),
                pl.BlockSpec(((ML_CONV - 1) * CONV_BLOCK, 2 * CONV_BLOCK), lambda b, h, lg: (0, 0))]
    rt_specs = [pspec(qb, OFF_RT_Q), pspec(qb, OFF_RT_K), pspec(vb, OFF_RT_V), pspec(vb, OFF_RT_Z),
                pl.BlockSpec((seq, qb), lambda b, h, lg: (b, 0)),
                pl.BlockSpec((seq, qb), lambda b, h, lg: (b, 0)),
                pl.BlockSpec((1, vb), lambda b, h, lg: (0, h))]
    assert len(ml_specs) == N_ML_IN and len(rt_specs) == N_RT_IN
    ml_scratch = [pltpu.VMEM((seq + CONV_BLOCK, 2 * qb), BF16),
                  pltpu.VMEM((seq, qb), F32),
                  pltpu.VMEM((seq, qb), F32),
                  pltpu.VMEM((nchunk, CHUNK), F32),
                  pltpu.VMEM((nchunk, CHUNK), F32),
                  pltpu.VMEM((nchunk, CHUNK), F32),
                  pltpu.VMEM((nchunk, CHUNK), F32),
                  pltpu.VMEM((nchunk, 1), F32),
                  pltpu.VMEM((nchunk, 1), F32),
                  pltpu.VMEM((nchunk, 1), F32),
                  pltpu.VMEM((qb, vb + LANES), F32)]
    assert len(ml_scratch) == N_ML_SCRATCH
    out_spec = pl.BlockSpec((seq, vb), lambda b, h, lg: (b, h))
    grid_spec = pltpu.PrefetchScalarGridSpec(
        num_scalar_prefetch=1,
        grid=(batch, ML_HEADS),
        in_specs=ml_specs + rt_specs,
        out_specs=[out_spec, out_spec],
        scratch_shapes=ml_scratch + [pltpu.VMEM((qb, vb), F32)],
    )
    return pl.pallas_call(
        functools.partial(_mixers_kernel, seq=seq),
        grid_spec=grid_spec,
        out_shape=[jax.ShapeDtypeStruct((t, ML_V), BF16), jax.ShapeDtypeStruct((t, RET_V), BF16)],
        compiler_params=_params(("arbitrary", "arbitrary")),
        name="mixers",
    )(log_gamma, p, p, p, p, p, gates3, gates3, conv_w, conv_w, conv_b, conv_b, ml_g,
      jnp.asarray(_conv_shift_matrix(), BF16), p, p, p, p, cos2, sin2, rt_g)


def _xattn_kernel(q_ref, z_ref, mk_ref, mv_ref, out_ref, *, blk):
    mk = mk_ref[...]
    mv = mv_ref[...]

    def block(i, carry):
        rows = pl.ds(pl.multiple_of(i * blk, blk), blk)
        sc = lax.dot_general(q_ref[rows, :], mk, (((1,), (1,)), ((), ())), preferred_element_type=F32)
        sc = sc * (XA_DH ** -0.5)
        e = jnp.exp(sc - jnp.max(sc, axis=-1, keepdims=True))
        o = _dot(e.astype(BF16), mv) * (1.0 / jnp.sum(e, axis=-1, keepdims=True))
        zh = 0.5 * z_ref[rows, :].astype(F32)
        out_ref[rows, :] = (o * (zh * (1.0 + jnp.tanh(zh)))).astype(BF16)
        return carry

    lax.fori_loop(0, q_ref.shape[0] // blk, block, 0, unroll=2)


def _xattn(p, memkv, batch, seq):
    t = batch * seq
    ts = seq
    ns = seq // ts
    w = XA_DH
    return pl.pallas_call(
        functools.partial(_xattn_kernel, blk=min(1024, ts)),
        grid=(batch, XA_HEADS, ns),
        in_specs=[pl.BlockSpec((ts, w), lambda b, h, s: (b * ns + s, OFF_XA_Q // w + h)),
                  pl.BlockSpec((ts, w), lambda b, h, s: (b * ns + s, OFF_XA_Z // w + h)),
                  pl.BlockSpec((MEM_TOKENS, w), lambda b, h, s: (b, h)),
                  pl.BlockSpec((MEM_TOKENS, w), lambda b, h, s: (b, XA_HEADS + h))],
        out_specs=pl.BlockSpec((ts, w), lambda b, h, s: (b * ns + s, h)),
        out_shape=jax.ShapeDtypeStruct((t, XA_W), BF16),
        compiler_params=_params(("arbitrary", "arbitrary", "arbitrary")),
        name="xattn",
    )(p, p, memkv, memkv)


def _merge_kernel(ml_ref, rt_ref, xa_ref, wml_ref, wrt_ref, wxa_ref, g0_ref, g1_ref, g2_ref, o_ref):
    acc = _sigmoid(g0_ref[...].astype(F32)) * _dot(ml_ref[...], wml_ref[...])
    acc = acc + _sigmoid(g1_ref[...].astype(F32)) * _dot(rt_ref[...], wrt_ref[...])
    acc = acc + _sigmoid(g2_ref[...].astype(F32)) * _dot(xa_ref[...], wxa_ref[...])
    o_ref[...] = acc.astype(BF16)


def _merge(ml_out, rt_out, xa_out, w_ml, w_rt, w_xa, p, tm, tn):
    t = ml_out.shape[0]
    nj = D_MODEL // tn

    def gspec(br):
        return pl.BlockSpec((tm, tn), lambda i, j, o=(OFF_GATE + br * D_MODEL) // tn: (i, o + j))

    return pl.pallas_call(
        _merge_kernel,
        grid=(t // tm, nj),
        in_specs=[pl.BlockSpec((tm, ML_V), lambda i, j: (i, 0)),
                  pl.BlockSpec((tm, RET_V), lambda i, j: (i, 0)),
                  pl.BlockSpec((tm, XA_W), lambda i, j: (i, 0)),
                  pl.BlockSpec((ML_V, tn), lambda i, j: (0, j)),
                  pl.BlockSpec((RET_V, tn), lambda i, j: (0, j)),
                  pl.BlockSpec((XA_W, tn), lambda i, j: (0, j)),
                  gspec(0), gspec(1), gspec(2)],
        out_specs=pl.BlockSpec((tm, tn), lambda i, j: (i, j)),
        out_shape=jax.ShapeDtypeStruct((t, D_MODEL), BF16),
        compiler_params=_params(("arbitrary", "arbitrary")),
        name="merge",
    )(ml_out, rt_out, xa_out, w_ml, w_rt, w_xa, p, p, p)


def _outproj_kernel(x_ref, m_ref, w_ref, g_ref, o_ref):
    y = x_ref[...] + _dot(m_ref[...], w_ref[...])
    o_ref[...] = _rms(y, g_ref[...])


def _outproj(x2, merged, w_out, final_g, tm):
    t = x2.shape[0]
    return pl.pallas_call(
        _outproj_kernel,
        grid=(t // tm,),
        in_specs=[pl.BlockSpec((tm, D_MODEL), lambda i: (i, 0)),
                  pl.BlockSpec((tm, D_MODEL), lambda i: (i, 0)),
                  pl.BlockSpec((D_MODEL, D_MODEL), lambda i: (0, 0)),
                  pl.BlockSpec((1, D_MODEL), lambda i: (0, 0))],
        out_specs=pl.BlockSpec((tm, D_MODEL), lambda i: (i, 0)),
        out_shape=jax.ShapeDtypeStruct((t, D_MODEL), F32),
        compiler_params=_params(("arbitrary",)),
        name="outproj",
    )(x2, merged, w_out, final_g)


def kernel(x, mem, positions, ln_g, mem_ln_g, w_in, b_in, conv_w, conv_b, ml_hnorm_g, ret_hnorm_g, w_mem_kv,
           w_br_ml, w_br_ret, w_br_xa, w_out, final_g):
    batch, seq, _ = x.shape
    assert seq % CHUNK == 0 and ln_g.shape[0] == 1
    t = batch * seq
    x2 = x.reshape(t, D_MODEL)

    wt, b0 = w_in[0].T, b_in[0]
    w_full = wt.astype(BF16)
    w_main = lax.dynamic_update_slice(w_full, w_full[OFF_IF + 2 * ML_HEADS:], (OFF_IF, 0))
    b_main = jnp.concatenate([b0[:OFF_IF], b0[OFF_IF + 2 * ML_HEADS:]])[None, :]
    wif_t = jnp.zeros((GATE_ROWS, D_MODEL), F32)
    wif_t = wif_t.at[0:ML_HEADS].set(wt[OFF_IF:OFF_IF + ML_HEADS])
    wif_t = wif_t.at[F_ROW:F_ROW + ML_HEADS].set(wt[OFF_IF + ML_HEADS:OFF_IF + 2 * ML_HEADS]).astype(BF16)
    bif_t = jnp.zeros((GATE_ROWS, 1), F32)
    bif_t = bif_t.at[0:ML_HEADS, 0].set(b0[OFF_IF:OFF_IF + ML_HEADS])
    bif_t = bif_t.at[F_ROW:F_ROW + ML_HEADS, 0].set(b0[OFF_IF + ML_HEADS:OFF_IF + 2 * ML_HEADS])

    memkv = _memkv(mem.reshape(batch * MEM_TOKENS, D_MODEL), mem_ln_g[0][None, :], w_mem_kv[0].astype(BF16))

    tm = min(1024, t)
    p, gates_t = _inproj(x2, ln_g[0][None, :], w_main, b_main, wif_t, bif_t, min(2048, t), 1024, 2)
    gates3 = gates_t.reshape(GATE_ROWS, t // CHUNK, CHUNK)

    half = RET_DQK // 2
    freqs = ROPE_BASE ** (-jnp.arange(half, dtype=F32) / half)
    freq2 = jnp.concatenate([freqs, freqs])[None, :]
    cos2, sin2 = _rope_tables(positions.reshape(2, t // 2).T, freq2)
    log_gamma = jnp.asarray(np.log(1.0 - 2.0 ** (-5.0 - np.arange(RET_HEADS))), dtype=F32)

    ml_out, rt_out = _mixers(log_gamma, p, gates3, conv_w[0], conv_b[0][None, :], ml_hnorm_g[0][None, :],
                             cos2, sin2, ret_hnorm_g[0][None, :], batch, seq)
    xa_out = _xattn(p, memkv, batch, seq)

    merged = _merge(ml_out, rt_out, xa_out, w_br_ml[0].astype(BF16), w_br_ret[0].astype(BF16),
                    w_br_xa[0].astype(BF16), p, tm, 512)
    out = _outproj(x2, merged, w_out[0].astype(BF16), final_g[None, :], min(512, t))
    return out.reshape(batch, seq, D_MODEL)
```

```python
import functools

import numpy as np
import jax
import jax.numpy as jnp
from jax import lax
from jax.experimental import pallas as pl
from jax.experimental.pallas import tpu as pltpu

F32 = jnp.float32
BF16 = jnp.bfloat16

D_MODEL = 2048
MEM_TOKENS = 256
ML_HEADS = 6
ML_DQK = 128
ML_DV = 256
ML_CONV = 4
RET_HEADS = 6
RET_DQK = 128
RET_DV = 256
XA_HEADS = 4
XA_DH = 256
ROPE_BASE = 10000.0
EPS = 1e-6
N_BRANCH = 3

ML_QK = ML_HEADS * ML_DQK
ML_V = ML_HEADS * ML_DV
RET_QK = RET_HEADS * RET_DQK
RET_V = RET_HEADS * RET_DV
XA_W = XA_HEADS * XA_DH

OFF_ML_Q = 0
OFF_ML_K = OFF_ML_Q + ML_QK
OFF_ML_V = OFF_ML_K + ML_QK
OFF_ML_O = OFF_ML_V + ML_V
OFF_ML_Z = OFF_ML_O + ML_V
OFF_IF = OFF_ML_Z + ML_V
OFF_RT_Q = OFF_ML_Z + ML_V
OFF_RT_K = OFF_RT_Q + RET_QK
OFF_RT_V = OFF_RT_K + RET_QK
OFF_RT_Z = OFF_RT_V + RET_V
OFF_XA_Q = OFF_RT_Z + RET_V
OFF_XA_Z = OFF_XA_Q + XA_W
OFF_GATE = OFF_XA_Z + XA_W
N_MAIN = OFF_GATE + N_BRANCH * D_MODEL

CHUNK = 256
GATE_ROWS = 16
F_ROW = 8
CONV_BLOCK = 128

LANES = 128
VMEM_LIMIT = 56 * 1024 * 1024

INPROJ_ROWS, INPROJ_COLS = 2048, 1024
INPROJ_NORM_STEPS = 2
MERGE_ROWS, MERGE_COLS = 1024, 512
OUTPROJ_ROWS = 512
MEMKV_ROWS = 512
ROPE_ROWS = 2048
XATTN_BLOCK = 1024


def _params(sem):
    return pltpu.CompilerParams(dimension_semantics=sem, vmem_limit_bytes=VMEM_LIMIT)


def _sigmoid(x):
    return 0.5 * jnp.tanh(0.5 * x) + 0.5


def _rms(xf, g):
    return xf * lax.rsqrt(jnp.mean(xf * xf, axis=-1, keepdims=True) + EPS) * g


def _head_norm(t, g):
    mu = jnp.mean(t, axis=-1, keepdims=True)
    c = t - mu
    var = jnp.mean(c * c, axis=-1, keepdims=True)
    return c * lax.rsqrt(var + EPS) * g


def _dot(a, b):
    return jnp.dot(a, b, preferred_element_type=F32)


def _memkv_kernel(m_ref, g_ref, w_ref, o_ref):
    h = _rms(m_ref[...], g_ref[...])
    o_ref[...] = _dot(h.astype(BF16), w_ref[...]).astype(BF16)


def _memkv(mem2, g, w):
    rows = mem2.shape[0]
    tm = min(MEMKV_ROWS, rows)
    return pl.pallas_call(
        _memkv_kernel,
        grid=(rows // tm,),
        in_specs=[pl.BlockSpec((tm, D_MODEL), lambda i: (i, 0)),
                  pl.BlockSpec((1, D_MODEL), lambda i: (0, 0)),
                  pl.BlockSpec((D_MODEL, 2 * XA_W), lambda i: (0, 0))],
        out_specs=pl.BlockSpec((tm, 2 * XA_W), lambda i: (i, 0)),
        out_shape=jax.ShapeDtypeStruct((rows, 2 * XA_W), BF16),
        compiler_params=_params(("arbitrary",)),
        name="memkv",
    )(mem2, g, w)


def _inproj_kernel(x_ref, g_ref, w_ref, b_ref, wif_ref, bif_ref, p_ref, gate_ref, hb_ref, *, nsplit, tail):
    j = pl.program_id(1)
    last = pl.num_programs(1) - 1
    tx = x_ref.shape[0]
    tn = w_ref.shape[0]

    def project(width):
        acc = lax.dot_general(hb_ref[...], w_ref[pl.ds(0, width), :], (((1,), (1,)), ((), ())),
                              preferred_element_type=F32)
        p_ref[:, pl.ds(0, width)] = (acc + b_ref[:, pl.ds(0, width)]).astype(BF16)

    @pl.when(j < nsplit)
    def _():
        hb = _rms(x_ref[...], g_ref[...]).astype(BF16)
        r0 = pl.multiple_of(j * tx, tx)
        hb_ref[pl.ds(r0, tx), :] = hb
        gt = lax.dot_general(wif_ref[...], hb, (((1,), (1,)), ((), ())), preferred_element_type=F32)
        gate_ref[:, pl.ds(r0, tx)] = gt + bif_ref[...]

    if tail == tn:
        pl.when(j >= nsplit)(lambda: project(tn))
    else:
        pl.when(jnp.logical_and(j >= nsplit, j < last))(lambda: project(tn))
        pl.when(j == last)(lambda: project(tail))


def _inproj(x2, g, w_main, b_main, wif_t, bif_t, tm, tn, nsplit):
    t = x2.shape[0]
    tx = tm // nsplit
    ncol = pl.cdiv(N_MAIN, tn)
    tail = N_MAIN - (ncol - 1) * tn

    def col(j):
        return jnp.maximum(j - nsplit, 0)

    return pl.pallas_call(
        functools.partial(_inproj_kernel, nsplit=nsplit, tail=tail),
        grid=(t // tm, nsplit + ncol),
        in_specs=[pl.BlockSpec((tx, D_MODEL), lambda i, j: (i * nsplit + jnp.minimum(j, nsplit - 1), 0)),
                  pl.BlockSpec((1, D_MODEL), lambda i, j: (0, 0)),
                  pl.BlockSpec((tn, D_MODEL), lambda i, j: (col(j), 0)),
                  pl.BlockSpec((1, tn), lambda i, j: (0, col(j))),
                  pl.BlockSpec((GATE_ROWS, D_MODEL), lambda i, j: (0, 0)),
                  pl.BlockSpec((GATE_ROWS, 1), lambda i, j: (0, 0))],
        out_specs=[pl.BlockSpec((tm, tn), lambda i, j: (i, col(j))),
                   pl.BlockSpec((GATE_ROWS, tm), lambda i, j: (0, i))],
        out_shape=[jax.ShapeDtypeStruct((t, N_MAIN), BF16),
                   jax.ShapeDtypeStruct((GATE_ROWS, t), F32)],
        scratch_shapes=[pltpu.VMEM((tm, D_MODEL), BF16)],
        compiler_params=_params(("arbitrary", "arbitrary")),
        name="inproj",
    )(x2, g, w_main, b_main, wif_t, bif_t)


def _rope_kernel(pos_ref, freq_ref, cos_ref, sin_ref):
    half = RET_DQK // 2
    low = lax.broadcasted_iota(jnp.int32, (1, RET_DQK), 1) < half
    pos = jnp.where(low, pos_ref[:, 0:1], pos_ref[:, 1:2]).astype(F32)
    ang = pos * freq_ref[...]
    c = jnp.cos(ang)
    s = jnp.sin(ang)
    c_other = pltpu.roll(c, half, 1)
    s_other = pltpu.roll(s, half, 1)
    cos_ref[0] = jnp.where(low, c, c_other)
    cos_ref[1] = jnp.where(low, c_other, c)
    sin_ref[0] = jnp.where(low, -s, s_other)
    sin_ref[1] = jnp.where(low, -s_other, s)


def _rope_tables(pos_pair, freq2):
    th = pos_pair.shape[0]
    tm = min(ROPE_ROWS, th)
    out_spec = pl.BlockSpec((2, tm, RET_DQK), lambda i: (0, i, 0))
    out_shape = jax.ShapeDtypeStruct((2, th, RET_DQK), F32)
    cos2, sin2 = pl.pallas_call(
        _rope_kernel,
        grid=(th // tm,),
        in_specs=[pl.BlockSpec((tm, 2), lambda i: (i, 0)),
                  pl.BlockSpec((1, RET_DQK), lambda i: (0, 0))],
        out_specs=[out_spec, out_spec],
        out_shape=[out_shape, out_shape],
        compiler_params=_params(("arbitrary",)),
        name="rope",
    )(pos_pair, freq2)
    return cos2.reshape(2 * th, RET_DQK), sin2.reshape(2 * th, RET_DQK)


def _split3(x):
    hi = x.astype(BF16)
    r1 = x - hi.astype(F32)
    mid = r1.astype(BF16)
    lo = (r1 - mid.astype(F32)).astype(BF16)
    return hi, mid, lo


def _conv_shift_matrix():
    s = np.zeros(((ML_CONV - 1) * CONV_BLOCK, 2 * CONV_BLOCK), np.float32)
    t = np.arange(CONV_BLOCK)
    for j in range(ML_CONV - 1):
        s[j * CONV_BLOCK + t, CONV_BLOCK + t - (ML_CONV - 1) + j] = 1.0
    return s


def _mlstm_setup(q_ref, k_ref, v_ref, o_ref, z_ref, gi_ref, gf_ref, cwq_ref, cwk_ref, cbq_ref, cbk_ref,
                 hg_ref, shift_ref, out_ref, pad_ref, qc_ref, kc_ref, u_ref, mr_ref, a_ref, w_ref, mp_ref, mn_ref,
                 dec_ref, caug_ref, *, seq):
    L = CHUNK
    nchunk = seq // L

    pad_ref[pl.ds(0, CONV_BLOCK), :] = jnp.zeros((CONV_BLOCK, 2 * ML_DQK), BF16)
    pad_ref[pl.ds(CONV_BLOCK, seq), pl.ds(0, ML_DQK)] = q_ref[...]
    pad_ref[pl.ds(CONV_BLOCK, seq), pl.ds(ML_DQK, ML_DQK)] = k_ref[...]
    cw = jnp.concatenate([cwq_ref[...], cwk_ref[...]], axis=-1)
    cb = jnp.concatenate([cbq_ref[...], cbk_ref[...]], axis=-1)
    shift = shift_ref[...]

    def conv_block(i, carry):
        r = pl.multiple_of(i * CONV_BLOCK, CONV_BLOCK)
        taps = _dot(shift, pad_ref[pl.ds(r, 2 * CONV_BLOCK), :])
        y = cb + pad_ref[pl.ds(r + CONV_BLOCK, CONV_BLOCK), :].astype(F32) * cw[ML_CONV - 1:ML_CONV, :]
        for j in range(ML_CONV - 1):
            y = y + taps[j * CONV_BLOCK:(j + 1) * CONV_BLOCK, :] * cw[j:j + 1, :]
        yh = 0.5 * y
        y = yh * (1.0 + jnp.tanh(yh))
        qc_ref[pl.ds(r, CONV_BLOCK), :] = y[:, :ML_DQK] * (ML_DQK ** -0.5)
        kc_ref[pl.ds(r, CONV_BLOCK), :] = y[:, ML_DQK:]
        return carry

    lax.fori_loop(0, seq // CONV_BLOCK, conv_block, 0, unroll=16)

    f = gf_ref[0]
    lf = jnp.minimum(f, 0.0) - jnp.log1p(jnp.exp(-jnp.abs(f)))
    row = lax.broadcasted_iota(jnp.int32, (L, L), 0)
    col = lax.broadcasted_iota(jnp.int32, (L, L), 1)
    triu = (row <= col).astype(BF16)
    hi, mid, lo = _split3(lf)
    bb = _dot(hi, triu) + _dot(mid, triu) + _dot(lo, triu)
    causal = row >= col
    diag = row == col

    gi = gi_ref[0]
    g_col = bb[:, L - 1:L]
    log_w = g_col - bb + gi
    w_max = jnp.max(log_w, axis=-1, keepdims=True)
    m = jnp.zeros((1, 1), F32)
    for c in range(nchunk):
        mp_ref[c:c + 1, :] = m
        m = jnp.maximum(g_col[c:c + 1, :] + m, w_max[c:c + 1, :])
        mn_ref[c:c + 1, :] = m
    m_prev_all = mp_ref[...]
    m_new_all = mn_ref[...]
    a_all = gi - bb
    a_ref[...] = a_all
    w_ref[...] = jnp.exp(log_w - m_new_all)
    dec_ref[...] = jnp.exp(g_col + m_prev_all - m_new_all)
    lane = lax.broadcasted_iota(jnp.int32, a_all.shape, 1)
    run = a_all
    shift_by = 1
    while shift_by < L:
        run = jnp.maximum(run, jnp.where(lane >= shift_by, pltpu.roll(run, shift_by, 1), -jnp.inf))
        shift_by *= 2
    u_all = jnp.maximum(m_prev_all, run)
    u_ref[...] = u_all
    mr_ref[...] = bb + u_all
    eye3 = jnp.tile(diag.astype(BF16), (1, 3))

    caug_ref[...] = jnp.zeros_like(caug_ref)
    hg_half = 0.5 * hg_ref[...]
    ones_blk = jnp.ones((L, LANES), BF16)

    def chunk(c):
        r0 = pl.multiple_of(c * L, L)
        rows = pl.ds(r0, L)
        one = pl.ds(c, 1)
        a_row = a_ref[one, :]
        w_row = w_ref[one, :]
        m_prev = mp_ref[one, :]
        decay = dec_ref[one, :]
        u_pieces = jnp.concatenate(_split3(u_ref[one, :]), axis=1)
        m_pieces = jnp.concatenate(_split3(mr_ref[one, :]), axis=1)
        rows_t = jnp.concatenate([jnp.broadcast_to(u_pieces, (LANES, 3 * L)),
                                  jnp.broadcast_to(m_pieces, (LANES, 3 * L))], axis=0)
        um = lax.dot_general(eye3, rows_t, (((1,), (1,)), ((), ())), preferred_element_type=F32)
        u_rep = um[:, :LANES]
        d = jnp.exp(jnp.where(causal, a_row, -jnp.inf) - jnp.tile(u_rep, (1, L // LANES)))
        inter = jnp.exp(m_prev - u_rep)
        qf = qc_ref[rows, :]
        kf = kc_ref[rows, :]
        qb = qf.astype(BF16)
        kt = kf.T
        vb = v_ref[rows, :]
        s = _dot(qb, kt.astype(BF16)) * d
        caug = caug_ref[...]
        vaug = jnp.concatenate([vb, ones_blk], axis=1)
        na = _dot(s.astype(BF16), vaug) + jnp.tile(inter, (1, ML_DV // LANES + 1)) * _dot(qb, caug.astype(BF16))
        den = jnp.maximum(jnp.abs(na[:, ML_DV:]), jnp.exp(-um[:, LANES:]))

        caug_ref[...] = decay * caug + _dot((kt * w_row).astype(BF16), vaug)

        num = na[:, :ML_DV]
        cen = num - jnp.mean(num, axis=-1, keepdims=True)
        var = jnp.mean(cen * cen, axis=-1, keepdims=True)
        y = cen * jnp.tile(lax.rsqrt(var + EPS * (den * den)), (1, ML_DV // LANES)) * hg_half
        oh = 0.5 * o_ref[rows, :].astype(F32)
        zh = 0.5 * z_ref[rows, :].astype(F32)
        y = y * (1.0 + jnp.tanh(oh)) * (zh * (1.0 + jnp.tanh(zh)))
        out_ref[rows, :] = y.astype(BF16)

    return chunk


def _ret_setup(lg_ref, q_ref, k_ref, v_ref, z_ref, cos_ref, sin_ref, hg_ref, out_ref, r_ref):
    L = CHUNK
    lg = lg_ref[pl.program_id(1)]
    row = lax.broadcasted_iota(jnp.int32, (L, L), 0)
    col = lax.broadcasted_iota(jnp.int32, (L, L), 1)
    intra = jnp.where(row >= col, jnp.exp((row - col).astype(F32) * lg), 0.0)
    pos_col = lax.broadcasted_iota(jnp.int32, (L, 1), 0).astype(F32)
    pos_row = lax.broadcasted_iota(jnp.int32, (1, L), 1).astype(F32)
    q_decay = jnp.exp((pos_col + 1.0) * lg)
    k_decay = jnp.exp((L - 1.0 - pos_row) * lg)
    chunk_decay = jnp.exp(jnp.full((1, 1), float(L), F32) * lg)
    r_ref[...] = jnp.zeros_like(r_ref)
    half = RET_DQK // 2
    hg = hg_ref[...]

    def chunk(c):
        r0 = pl.multiple_of(c * L, L)
        rows = pl.ds(r0, L)
        cs = cos_ref[rows, :]
        sn = sin_ref[rows, :]
        qf = q_ref[rows, :].astype(F32)
        kf = k_ref[rows, :].astype(F32)
        qr = qf * cs + pltpu.roll(qf, half, 1) * sn
        kr = (kf * cs + pltpu.roll(kf, half, 1) * sn) * (RET_DQK ** -0.5)
        qb = qr.astype(BF16)
        kt = kr.T
        vb = v_ref[rows, :]
        s = _dot(qb, kt.astype(BF16)) * intra
        r_state = r_ref[...]
        o = _dot(s.astype(BF16), vb) + q_decay * _dot(qb, r_state.astype(BF16))
        r_ref[...] = chunk_decay * r_state + _dot((kt * k_decay).astype(BF16), vb)
        y = _head_norm(o, hg)
        zh = 0.5 * z_ref[rows, :].astype(F32)
        out_ref[rows, :] = (y * (zh * (1.0 + jnp.tanh(zh)))).astype(BF16)

    return chunk


N_ML_IN = 13
N_RT_IN = 7
N_ML_SCRATCH = 11
CHUNK_UNROLL = 4


def _mixers_kernel(lg_ref, *refs, seq):
    ml_in = refs[:N_ML_IN]
    rt_in = refs[N_ML_IN:N_ML_IN + N_RT_IN]
    ml_out, rt_out = refs[N_ML_IN + N_RT_IN:N_ML_IN + N_RT_IN + 2]
    scratch = refs[N_ML_IN + N_RT_IN + 2:]
    nchunk = seq // CHUNK
    ml_chunk = _mlstm_setup(*ml_in, ml_out, *scratch[:N_ML_SCRATCH], seq=seq)
    lax.fori_loop(0, nchunk, lambda c, carry: (ml_chunk(c), carry)[1], 0, unroll=CHUNK_UNROLL)
    rt_chunk = _ret_setup(lg_ref, *rt_in, rt_out, scratch[N_ML_SCRATCH])
    lax.fori_loop(0, nchunk, lambda c, carry: (rt_chunk(c), carry)[1], 0, unroll=CHUNK_UNROLL)


def _mixers(log_gamma, p, gates3, conv_w, conv_b, ml_g, cos2, sin2, rt_g, batch, seq):
    t = batch * seq
    nchunk = seq // CHUNK
    qb, vb = ML_DQK, ML_DV
    assert (ML_HEADS, ML_DQK, ML_DV) == (RET_HEADS, RET_DQK, RET_DV)

    def pspec(width, off):
        return pl.BlockSpec((seq, width), lambda b, h, lg, off=off // width: (b, off + h))

    def gspec(row0):
        return pl.BlockSpec((1, nchunk, CHUNK), lambda b, h, lg, row0=row0: (row0 + h, b, 0))

    ml_specs = [pspec(qb, OFF_ML_Q), pspec(qb, OFF_ML_K), pspec(vb, OFF_ML_V), pspec(vb, OFF_ML_O),
                pspec(vb, OFF_ML_Z), gspec(0), gspec(F_ROW),
                pl.BlockSpec((ML_CONV, qb), lambda b, h, lg: (0, h)),
                pl.BlockSpec((ML_CONV, qb), lambda b, h, lg: (0, ML_HEADS + h)),
                pl.BlockSpec((1, qb), lambda b, h, lg: (0, h)),
                pl.BlockSpec((1, qb), lambda b, h, lg: (0, ML_HEADS + h)),
                pl.BlockSpec((1, vb), lambda b, h, lg: (0, h)),
                pl.BlockSpec(((ML_CONV - 1) * CONV_BLOCK, 2 * CONV_BLOCK), lambda b, h, lg: (0, 0))]
    rt_specs = [pspec(qb, OFF_RT_Q), pspec(qb, OFF_RT_K), pspec(vb, OFF_RT_V), pspec(vb, OFF_RT_Z),
                pl.BlockSpec((seq, qb), lambda b, h, lg: (b, 0)),
                pl.BlockSpec((seq, qb), lambda b, h, lg: (b, 0)),
                pl.BlockSpec((1, vb), lambda b, h, lg: (0, h))]
    assert len(ml_specs) == N_ML_IN and len(rt_specs) == N_RT_IN
    ml_scratch = [pltpu.VMEM((seq + CONV_BLOCK, 2 * qb), BF16),
                  pltpu.VMEM((seq, qb), F32),
                  pltpu.VMEM((seq, qb), F32),
                  pltpu.VMEM((nchunk, CHUNK), F32),
                  pltpu.VMEM((nchunk, CHUNK), F32),
                  pltpu.VMEM((nchunk, CHUNK), F32),
                  pltpu.VMEM((nchunk, CHUNK), F32),
                  pltpu.VMEM((nchunk, 1), F32),
                  pltpu.VMEM((nchunk, 1), F32),
                  pltpu.VMEM((nchunk, 1), F32),
                  pltpu.VMEM((qb, vb + LANES), F32)]
    assert len(ml_scratch) == N_ML_SCRATCH
    out_spec = pl.BlockSpec((seq, vb), lambda b, h, lg: (b, h))
    grid_spec = pltpu.PrefetchScalarGridSpec(
        num_scalar_prefetch=1,
        grid=(batch, ML_HEADS),
        in_specs=ml_specs + rt_specs,
        out_specs=[out_spec, out_spec],
        scratch_shapes=ml_scratch + [pltpu.VMEM((qb, vb), F32)],
    )
    return pl.pallas_call(
        functools.partial(_mixers_kernel, seq=seq),
        grid_spec=grid_spec,
        out_shape=[jax.ShapeDtypeStruct((t, ML_V), BF16), jax.ShapeDtypeStruct((t, RET_V), BF16)],
        compiler_params=_params(("arbitrary", "arbitrary")),
        name="mixers",
    )(log_gamma, p, p, p, p, p, gates3, gates3, conv_w, conv_w, conv_b, conv_b, ml_g,
      jnp.asarray(_conv_shift_matrix(), BF16), p, p, p, p, cos2, sin2, rt_g)


def _xattn_kernel(q_ref, z_ref, mk_ref, mv_ref, out_ref, *, blk):
    mk = mk_ref[...]
    mv = mv_ref[...]

    def block(i, carry):
        rows = pl.ds(pl.multiple_of(i * blk, blk), blk)
        sc = lax.dot_general(q_ref[rows, :], mk, (((1,), (1,)), ((), ())), preferred_element_type=F32)
        sc = sc * (XA_DH ** -0.5)
        e = jnp.exp(sc - jnp.max(sc, axis=-1, keepdims=True))
        o = _dot(e.astype(BF16), mv) * (1.0 / jnp.sum(e, axis=-1, keepdims=True))
        zh = 0.5 * z_ref[rows, :].astype(F32)
        out_ref[rows, :] = (o * (zh * (1.0 + jnp.tanh(zh)))).astype(BF16)
        return carry

    lax.fori_loop(0, q_ref.shape[0] // blk, block, 0, unroll=2)


def _xattn(p, memkv, batch, seq):
    t = batch * seq
    ts = seq
    ns = seq // ts
    w = XA_DH
    return pl.pallas_call(
        functools.partial(_xattn_kernel, blk=min(XATTN_BLOCK, ts)),
        grid=(batch, XA_HEADS, ns),
        in_specs=[pl.BlockSpec((ts, w), lambda b, h, s: (b * ns + s, OFF_XA_Q // w + h)),
                  pl.BlockSpec((ts, w), lambda b, h, s: (b * ns + s, OFF_XA_Z // w + h)),
                  pl.BlockSpec((MEM_TOKENS, w), lambda b, h, s: (b, h)),
                  pl.BlockSpec((MEM_TOKENS, w), lambda b, h, s: (b, XA_HEADS + h))],
        out_specs=pl.BlockSpec((ts, w), lambda b, h, s: (b * ns + s, h)),
        out_shape=jax.ShapeDtypeStruct((t, XA_W), BF16),
        compiler_params=_params(("arbitrary", "arbitrary", "arbitrary")),
        name="xattn",
    )(p, p, memkv, memkv)


def _merge_kernel(ml_ref, rt_ref, xa_ref, wml_ref, wrt_ref, wxa_ref, g0_ref, g1_ref, g2_ref, o_ref):
    acc = _sigmoid(g0_ref[...].astype(F32)) * _dot(ml_ref[...], wml_ref[...])
    acc = acc + _sigmoid(g1_ref[...].astype(F32)) * _dot(rt_ref[...], wrt_ref[...])
    acc = acc + _sigmoid(g2_ref[...].astype(F32)) * _dot(xa_ref[...], wxa_ref[...])
    o_ref[...] = acc.astype(BF16)


def _merge(ml_out, rt_out, xa_out, w_ml, w_rt, w_xa, p, tm, tn):
    t = ml_out.shape[0]
    nj = D_MODEL // tn

    def gspec(br):
        return pl.BlockSpec((tm, tn), lambda i, j, o=(OFF_GATE + br * D_MODEL) // tn: (i, o + j))

    return pl.pallas_call(
        _merge_kernel,
        grid=(t // tm, nj),
        in_specs=[pl.BlockSpec((tm, ML_V), lambda i, j: (i, 0)),
                  pl.BlockSpec((tm, RET_V), lambda i, j: (i, 0)),
                  pl.BlockSpec((tm, XA_W), lambda i, j: (i, 0)),
                  pl.BlockSpec((ML_V, tn), lambda i, j: (0, j)),
                  pl.BlockSpec((RET_V, tn), lambda i, j: (0, j)),
                  pl.BlockSpec((XA_W, tn), lambda i, j: (0, j)),
                  gspec(0), gspec(1), gspec(2)],
        out_specs=pl.BlockSpec((tm, tn), lambda i, j: (i, j)),
        out_shape=jax.ShapeDtypeStruct((t, D_MODEL), BF16),
        compiler_params=_params(("arbitrary", "arbitrary")),
        name="merge",
    )(ml_out, rt_out, xa_out, w_ml, w_rt, w_xa, p, p, p)


def _outproj_kernel(x_ref, m_ref, w_ref, g_ref, o_ref):
    y = x_ref[...] + _dot(m_ref[...], w_ref[...])
    o_ref[...] = _rms(y, g_ref[...])


def _outproj(x2, merged, w_out, final_g, tm):
    t = x2.shape[0]
    return pl.pallas_call(
        _outproj_kernel,
        grid=(t // tm,),
        in_specs=[pl.BlockSpec((tm, D_MODEL), lambda i: (i, 0)),
                  pl.BlockSpec((tm, D_MODEL), lambda i: (i, 0)),
                  pl.BlockSpec((D_MODEL, D_MODEL), lambda i: (0, 0)),
                  pl.BlockSpec((1, D_MODEL), lambda i: (0, 0))],
        out_specs=pl.BlockSpec((tm, D_MODEL), lambda i: (i, 0)),
        out_shape=jax.ShapeDtypeStruct((t, D_MODEL), F32),
        compiler_params=_params(("arbitrary",)),
        name="outproj",
    )(x2, merged, w_out, final_g)


def kernel(x, mem, positions, ln_g, mem_ln_g, w_in, b_in, conv_w, conv_b, ml_hnorm_g, ret_hnorm_g, w_mem_kv,
           w_br_ml, w_br_ret, w_br_xa, w_out, final_g):
    batch, seq, _ = x.shape
    assert seq % CHUNK == 0 and ln_g.shape[0] == 1
    t = batch * seq
    x2 = x.reshape(t, D_MODEL)

    wt, b0 = w_in[0].T, b_in[0]
    w_full = wt.astype(BF16)
    w_main = lax.dynamic_update_slice(w_full, w_full[OFF_IF + 2 * ML_HEADS:], (OFF_IF, 0))
    b_main = jnp.concatenate([b0[:OFF_IF], b0[OFF_IF + 2 * ML_HEADS:]])[None, :]
    wif_t = jnp.zeros((GATE_ROWS, D_MODEL), F32)
    wif_t = wif_t.at[0:ML_HEADS].set(wt[OFF_IF:OFF_IF + ML_HEADS])
    wif_t = wif_t.at[F_ROW:F_ROW + ML_HEADS].set(wt[OFF_IF + ML_HEADS:OFF_IF + 2 * ML_HEADS]).astype(BF16)
    bif_t = jnp.zeros((GATE_ROWS, 1), F32)
    bif_t = bif_t.at[0:ML_HEADS, 0].set(b0[OFF_IF:OFF_IF + ML_HEADS])
    bif_t = bif_t.at[F_ROW:F_ROW + ML_HEADS, 0].set(b0[OFF_IF + ML_HEADS:OFF_IF + 2 * ML_HEADS])

    memkv = _memkv(mem.reshape(batch * MEM_TOKENS, D_MODEL), mem_ln_g[0][None, :], w_mem_kv[0].astype(BF16))

    p, gates_t = _inproj(x2, ln_g[0][None, :], w_main, b_main, wif_t, bif_t, min(INPROJ_ROWS, t), INPROJ_COLS,
                         INPROJ_NORM_STEPS)
    gates3 = gates_t.reshape(GATE_ROWS, t // CHUNK, CHUNK)

    half = RET_DQK // 2
    freqs = ROPE_BASE ** (-jnp.arange(half, dtype=F32) / half)
    freq2 = jnp.concatenate([freqs, freqs])[None, :]
    cos2, sin2 = _rope_tables(positions.reshape(2, t // 2).T, freq2)
    log_gamma = jnp.asarray(np.log(1.0 - 2.0 ** (-5.0 - np.arange(RET_HEADS))), dtype=F32)

    ml_out, rt_out = _mixers(log_gamma, p, gates3, conv_w[0], conv_b[0][None, :], ml_hnorm_g[0][None, :],
                             cos2, sin2, ret_hnorm_g[0][None, :], batch, seq)
    xa_out = _xattn(p, memkv, batch, seq)

    merged = _merge(ml_out, rt_out, xa_out, w_br_ml[0].astype(BF16), w_br_ret[0].astype(BF16),
                    w_br_xa[0].astype(BF16), p, min(MERGE_ROWS, t), MERGE_COLS)
    out = _outproj(x2, merged, w_out[0].astype(BF16), final_g[None, :], min(OUTPROJ_ROWS, t))
    return out.reshape(batch, seq, D_MODEL)
```

```python
import functools

import numpy as np
import jax
import jax.numpy as jnp
from jax import lax
from jax.experimental import pallas as pl
from jax.experimental.pallas import tpu as pltpu

F32 = jnp.float32
BF16 = jnp.bfloat16

D_MODEL = 2048
MEM_TOKENS = 256
ML_HEADS = 6
ML_DQK = 128
ML_DV = 256
ML_CONV = 4
RET_HEADS = 6
RET_DQK = 128
RET_DV = 256
XA_HEADS = 4
XA_DH = 256
ROPE_BASE = 10000.0
EPS = 1e-6
N_BRANCH = 3

ML_QK = ML_HEADS * ML_DQK
ML_V = ML_HEADS * ML_DV
RET_QK = RET_HEADS * RET_DQK
RET_V = RET_HEADS * RET_DV
XA_W = XA_HEADS * XA_DH

OFF_ML_Q = 0
OFF_ML_K = OFF_ML_Q + ML_QK
OFF_ML_V = OFF_ML_K + ML_QK
OFF_ML_O = OFF_ML_V + ML_V
OFF_ML_Z = OFF_ML_O + ML_V
OFF_IF = OFF_ML_Z + ML_V
OFF_RT_Q = OFF_ML_Z + ML_V
OFF_RT_K = OFF_RT_Q + RET_QK
OFF_RT_V = OFF_RT_K + RET_QK
OFF_RT_Z = OFF_RT_V + RET_V
OFF_XA_Q = OFF_RT_Z + RET_V
OFF_XA_Z = OFF_XA_Q + XA_W
OFF_GATE = OFF_XA_Z + XA_W
N_MAIN = OFF_GATE + N_BRANCH * D_MODEL

CHUNK = 256
GATE_ROWS = 16
F_ROW = 8
CONV_BLOCK = 128

LANES = 128
VMEM_LIMIT = 56 * 1024 * 1024

INPROJ_ROWS, INPROJ_COLS = 2048, 1024
INPROJ_NORM_STEPS = 2
MERGE_ROWS, MERGE_COLS = 1024, 1024
MERGE_GATE_COLS = 512
OUTPROJ_ROWS = 512
MEMKV_ROWS = 512
ROPE_ROWS = 2048
XATTN_BLOCK = 1024


def _params(sem):
    return pltpu.CompilerParams(dimension_semantics=sem, vmem_limit_bytes=VMEM_LIMIT)


def _sigmoid(x):
    return 0.5 * jnp.tanh(0.5 * x) + 0.5


def _rms(xf, g):
    return xf * lax.rsqrt(jnp.mean(xf * xf, axis=-1, keepdims=True) + EPS) * g


def _head_norm(t, g):
    mu = jnp.mean(t, axis=-1, keepdims=True)
    c = t - mu
    var = jnp.mean(c * c, axis=-1, keepdims=True)
    return c * lax.rsqrt(var + EPS) * g


def _dot(a, b):
    return jnp.dot(a, b, preferred_element_type=F32)


def _memkv_kernel(m_ref, g_ref, w_ref, o_ref):
    h = _rms(m_ref[...], g_ref[...])
    o_ref[...] = _dot(h.astype(BF16), w_ref[...]).astype(BF16)


def _memkv(mem2, g, w):
    rows = mem2.shape[0]
    tm = min(MEMKV_ROWS, rows)
    return pl.pallas_call(
        _memkv_kernel,
        grid=(rows // tm,),
        in_specs=[pl.BlockSpec((tm, D_MODEL), lambda i: (i, 0)),
                  pl.BlockSpec((1, D_MODEL), lambda i: (0, 0)),
                  pl.BlockSpec((D_MODEL, 2 * XA_W), lambda i: (0, 0))],
        out_specs=pl.BlockSpec((tm, 2 * XA_W), lambda i: (i, 0)),
        out_shape=jax.ShapeDtypeStruct((rows, 2 * XA_W), BF16),
        compiler_params=_params(("arbitrary",)),
        name="memkv",
    )(mem2, g, w)


def _inproj_kernel(x_ref, g_ref, w_ref, b_ref, wif_ref, bif_ref, p_ref, gate_ref, hb_ref, *, nsplit, tail):
    j = pl.program_id(1)
    last = pl.num_programs(1) - 1
    tx = x_ref.shape[0]
    tn = w_ref.shape[0]

    def project(width):
        acc = lax.dot_general(hb_ref[...], w_ref[pl.ds(0, width), :], (((1,), (1,)), ((), ())),
                              preferred_element_type=F32)
        p_ref[:, pl.ds(0, width)] = (acc + b_ref[:, pl.ds(0, width)]).astype(BF16)

    @pl.when(j < nsplit)
    def _():
        hb = _rms(x_ref[...], g_ref[...]).astype(BF16)
        r0 = pl.multiple_of(j * tx, tx)
        hb_ref[pl.ds(r0, tx), :] = hb
        gt = lax.dot_general(wif_ref[...], hb, (((1,), (1,)), ((), ())), preferred_element_type=F32)
        gate_ref[:, pl.ds(r0, tx)] = gt + bif_ref[...]

    if tail == tn:
        pl.when(j >= nsplit)(lambda: project(tn))
    else:
        pl.when(jnp.logical_and(j >= nsplit, j < last))(lambda: project(tn))
        pl.when(j == last)(lambda: project(tail))


def _inproj(x2, g, w_main, b_main, wif_t, bif_t, tm, tn, nsplit):
    t = x2.shape[0]
    tx = tm // nsplit
    ncol = pl.cdiv(N_MAIN, tn)
    tail = N_MAIN - (ncol - 1) * tn

    def col(j):
        return jnp.maximum(j - nsplit, 0)

    return pl.pallas_call(
        functools.partial(_inproj_kernel, nsplit=nsplit, tail=tail),
        grid=(t // tm, nsplit + ncol),
        in_specs=[pl.BlockSpec((tx, D_MODEL), lambda i, j: (i * nsplit + jnp.minimum(j, nsplit - 1), 0)),
                  pl.BlockSpec((1, D_MODEL), lambda i, j: (0, 0)),
                  pl.BlockSpec((tn, D_MODEL), lambda i, j: (col(j), 0)),
                  pl.BlockSpec((1, tn), lambda i, j: (0, col(j))),
                  pl.BlockSpec((GATE_ROWS, D_MODEL), lambda i, j: (0, 0)),
                  pl.BlockSpec((GATE_ROWS, 1), lambda i, j: (0, 0))],
        out_specs=[pl.BlockSpec((tm, tn), lambda i, j: (i, col(j))),
                   pl.BlockSpec((GATE_ROWS, tm), lambda i, j: (0, i))],
        out_shape=[jax.ShapeDtypeStruct((t, N_MAIN), BF16),
                   jax.ShapeDtypeStruct((GATE_ROWS, t), F32)],
        scratch_shapes=[pltpu.VMEM((tm, D_MODEL), BF16)],
        compiler_params=_params(("arbitrary", "arbitrary")),
        name="inproj",
    )(x2, g, w_main, b_main, wif_t, bif_t)


def _rope_kernel(pos_ref, freq_ref, cos_ref, sin_ref):
    half = RET_DQK // 2
    low = lax.broadcasted_iota(jnp.int32, (1, RET_DQK), 1) < half
    pos = jnp.where(low, pos_ref[:, 0:1], pos_ref[:, 1:2]).astype(F32)
    ang = pos * freq_ref[...]
    c = jnp.cos(ang)
    s = jnp.sin(ang)
    c_other = pltpu.roll(c, half, 1)
    s_other = pltpu.roll(s, half, 1)
    cos_ref[0] = jnp.where(low, c, c_other)
    cos_ref[1] = jnp.where(low, c_other, c)
    sin_ref[0] = jnp.where(low, -s, s_other)
    sin_ref[1] = jnp.where(low, -s_other, s)


def _rope_tables(pos_pair, freq2):
    th = pos_pair.shape[0]
    tm = min(ROPE_ROWS, th)
    out_spec = pl.BlockSpec((2, tm, RET_DQK), lambda i: (0, i, 0))
    out_shape = jax.ShapeDtypeStruct((2, th, RET_DQK), F32)
    cos2, sin2 = pl.pallas_call(
        _rope_kernel,
        grid=(th // tm,),
        in_specs=[pl.BlockSpec((tm, 2), lambda i: (i, 0)),
                  pl.BlockSpec((1, RET_DQK), lambda i: (0, 0))],
        out_specs=[out_spec, out_spec],
        out_shape=[out_shape, out_shape],
        compiler_params=_params(("arbitrary",)),
        name="rope",
    )(pos_pair, freq2)
    return cos2.reshape(2 * th, RET_DQK), sin2.reshape(2 * th, RET_DQK)


def _split3(x):
    hi = x.astype(BF16)
    r1 = x - hi.astype(F32)
    mid = r1.astype(BF16)
    lo = (r1 - mid.astype(F32)).astype(BF16)
    return hi, mid, lo


def _conv_shift_matrix():
    s = np.zeros(((ML_CONV - 1) * CONV_BLOCK, 2 * CONV_BLOCK), np.float32)
    t = np.arange(CONV_BLOCK)
    for j in range(ML_CONV - 1):
        s[j * CONV_BLOCK + t, CONV_BLOCK + t - (ML_CONV - 1) + j] = 1.0
    return s


def _mlstm_setup(q_ref, k_ref, v_ref, o_ref, z_ref, gi_ref, gf_ref, cwq_ref, cwk_ref, cbq_ref, cbk_ref,
                 hg_ref, shift_ref, out_ref, pad_ref, qc_ref, kc_ref, u_ref, mr_ref, a_ref, w_ref, mp_ref, mn_ref,
                 dec_ref, caug_ref, *, seq):
    L = CHUNK
    nchunk = seq // L

    pad_ref[pl.ds(0, CONV_BLOCK), :] = jnp.zeros((CONV_BLOCK, 2 * ML_DQK), BF16)
    pad_ref[pl.ds(CONV_BLOCK, seq), pl.ds(0, ML_DQK)] = q_ref[...]
    pad_ref[pl.ds(CONV_BLOCK, seq), pl.ds(ML_DQK, ML_DQK)] = k_ref[...]
    cw = jnp.concatenate([cwq_ref[...], cwk_ref[...]], axis=-1)
    cb = jnp.concatenate([cbq_ref[...], cbk_ref[...]], axis=-1)
    shift = shift_ref[...]

    def conv_block(i, carry):
        r = pl.multiple_of(i * CONV_BLOCK, CONV_BLOCK)
        taps = _dot(shift, pad_ref[pl.ds(r, 2 * CONV_BLOCK), :])
        y = cb + pad_ref[pl.ds(r + CONV_BLOCK, CONV_BLOCK), :].astype(F32) * cw[ML_CONV - 1:ML_CONV, :]
        for j in range(ML_CONV - 1):
            y = y + taps[j * CONV_BLOCK:(j + 1) * CONV_BLOCK, :] * cw[j:j + 1, :]
        yh = 0.5 * y
        y = yh * (1.0 + jnp.tanh(yh))
        qc_ref[pl.ds(r, CONV_BLOCK), :] = y[:, :ML_DQK] * (ML_DQK ** -0.5)
        kc_ref[pl.ds(r, CONV_BLOCK), :] = y[:, ML_DQK:]
        return carry

    lax.fori_loop(0, seq // CONV_BLOCK, conv_block, 0, unroll=16)

    f = gf_ref[0]
    lf = jnp.minimum(f, 0.0) - jnp.log1p(jnp.exp(-jnp.abs(f)))
    row = lax.broadcasted_iota(jnp.int32, (L, L), 0)
    col = lax.broadcasted_iota(jnp.int32, (L, L), 1)
    triu = (row <= col).astype(BF16)
    hi, mid, lo = _split3(lf)
    bb = _dot(hi, triu) + _dot(mid, triu) + _dot(lo, triu)
    causal = row >= col
    diag = row == col

    gi = gi_ref[0]
    g_col = bb[:, L - 1:L]
    log_w = g_col - bb + gi
    w_max = jnp.max(log_w, axis=-1, keepdims=True)
    m = jnp.zeros((1, 1), F32)
    for c in range(nchunk):
        mp_ref[c:c + 1, :] = m
        m = jnp.maximum(g_col[c:c + 1, :] + m, w_max[c:c + 1, :])
        mn_ref[c:c + 1, :] = m
    m_prev_all = mp_ref[...]
    m_new_all = mn_ref[...]
    a_all = gi - bb
    a_ref[...] = a_all
    w_ref[...] = jnp.exp(log_w - m_new_all)
    dec_ref[...] = jnp.exp(g_col + m_prev_all - m_new_all)
    lane = lax.broadcasted_iota(jnp.int32, a_all.shape, 1)
    run = a_all
    shift_by = 1
    while shift_by < L:
        run = jnp.maximum(run, jnp.where(lane >= shift_by, pltpu.roll(run, shift_by, 1), -jnp.inf))
        shift_by *= 2
    u_all = jnp.maximum(m_prev_all, run)
    u_ref[...] = u_all
    mr_ref[...] = bb + u_all
    eye3 = jnp.tile(diag.astype(BF16), (1, 3))

    caug_ref[...] = jnp.zeros_like(caug_ref)
    hg_half = 0.5 * hg_ref[...]
    ones_blk = jnp.ones((L, LANES), BF16)

    def chunk(c):
        r0 = pl.multiple_of(c * L, L)
        rows = pl.ds(r0, L)
        one = pl.ds(c, 1)
        a_row = a_ref[one, :]
        w_row = w_ref[one, :]
        m_prev = mp_ref[one, :]
        decay = dec_ref[one, :]
        u_pieces = jnp.concatenate(_split3(u_ref[one, :]), axis=1)
        m_pieces = jnp.concatenate(_split3(mr_ref[one, :]), axis=1)
        rows_t = jnp.concatenate([jnp.broadcast_to(u_pieces, (LANES, 3 * L)),
                                  jnp.broadcast_to(m_pieces, (LANES, 3 * L))], axis=0)
        um = lax.dot_general(eye3, rows_t, (((1,), (1,)), ((), ())), preferred_element_type=F32)
        u_rep = um[:, :LANES]
        d = jnp.exp(jnp.where(causal, a_row, -jnp.inf) - jnp.tile(u_rep, (1, L // LANES)))
        inter = jnp.exp(m_prev - u_rep)
        qf = qc_ref[rows, :]
        kf = kc_ref[rows, :]
        qb = qf.astype(BF16)
        kt = kf.T
        vb = v_ref[rows, :]
        s = _dot(qb, kt.astype(BF16)) * d
        caug = caug_ref[...]
        vaug = jnp.concatenate([vb, ones_blk], axis=1)
        na = _dot(s.astype(BF16), vaug) + jnp.tile(inter, (1, ML_DV // LANES + 1)) * _dot(qb, caug.astype(BF16))
        den = jnp.maximum(jnp.abs(na[:, ML_DV:]), jnp.exp(-um[:, LANES:]))

        caug_ref[...] = decay * caug + _dot((kt * w_row).astype(BF16), vaug)

        num = na[:, :ML_DV]
        cen = num - jnp.mean(num, axis=-1, keepdims=True)
        var = jnp.mean(cen * cen, axis=-1, keepdims=True)
        y = cen * jnp.tile(lax.rsqrt(var + EPS * (den * den)), (1, ML_DV // LANES)) * hg_half
        oh = 0.5 * o_ref[rows, :].astype(F32)
        zh = 0.5 * z_ref[rows, :].astype(F32)
        y = y * (1.0 + jnp.tanh(oh)) * (zh * (1.0 + jnp.tanh(zh)))
        out_ref[rows, :] = y.astype(BF16)

    return chunk


def _ret_setup(lg_ref, q_ref, k_ref, v_ref, z_ref, cos_ref, sin_ref, hg_ref, out_ref, r_ref):
    L = CHUNK
    lg = lg_ref[pl.program_id(1)]
    row = lax.broadcasted_iota(jnp.int32, (L, L), 0)
    col = lax.broadcasted_iota(jnp.int32, (L, L), 1)
    intra = jnp.where(row >= col, jnp.exp((row - col).astype(F32) * lg), 0.0)
    pos_col = lax.broadcasted_iota(jnp.int32, (L, 1), 0).astype(F32)
    pos_row = lax.broadcasted_iota(jnp.int32, (1, L), 1).astype(F32)
    q_decay = jnp.exp((pos_col + 1.0) * lg)
    k_decay = jnp.exp((L - 1.0 - pos_row) * lg)
    chunk_decay = jnp.exp(jnp.full((1, 1), float(L), F32) * lg)
    r_ref[...] = jnp.zeros_like(r_ref)
    half = RET_DQK // 2
    hg = hg_ref[...]

    def chunk(c):
        r0 = pl.multiple_of(c * L, L)
        rows = pl.ds(r0, L)
        cs = cos_ref[rows, :]
        sn = sin_ref[rows, :]
        qf = q_ref[rows, :].astype(F32)
        kf = k_ref[rows, :].astype(F32)
        qr = qf * cs + pltpu.roll(qf, half, 1) * sn
        kr = (kf * cs + pltpu.roll(kf, half, 1) * sn) * (RET_DQK ** -0.5)
        qb = qr.astype(BF16)
        kt = kr.T
        vb = v_ref[rows, :]
        s = _dot(qb, kt.astype(BF16)) * intra
        r_state = r_ref[...]
        o = _dot(s.astype(BF16), vb) + q_decay * _dot(qb, r_state.astype(BF16))
        r_ref[...] = chunk_decay * r_state + _dot((kt * k_decay).astype(BF16), vb)
        y = _head_norm(o, hg)
        zh = 0.5 * z_ref[rows, :].astype(F32)
        out_ref[rows, :] = (y * (zh * (1.0 + jnp.tanh(zh)))).astype(BF16)

    return chunk


N_ML_IN = 13
N_RT_IN = 7
N_ML_SCRATCH = 11
CHUNK_UNROLL = 4


def _mixers_kernel(lg_ref, *refs, seq):
    ml_in = refs[:N_ML_IN]
    rt_in = refs[N_ML_IN:N_ML_IN + N_RT_IN]
    ml_out, rt_out = refs[N_ML_IN + N_RT_IN:N_ML_IN + N_RT_IN + 2]
    scratch = refs[N_ML_IN + N_RT_IN + 2:]
    nchunk = seq // CHUNK
    ml_chunk = _mlstm_setup(*ml_in, ml_out, *scratch[:N_ML_SCRATCH], seq=seq)
    lax.fori_loop(0, nchunk, lambda c, carry: (ml_chunk(c), carry)[1], 0, unroll=CHUNK_UNROLL)
    rt_chunk = _ret_setup(lg_ref, *rt_in, rt_out, scratch[N_ML_SCRATCH])
    lax.fori_loop(0, nchunk, lambda c, carry: (rt_chunk(c), carry)[1], 0, unroll=CHUNK_UNROLL)


def _mixers(log_gamma, p, gates3, conv_w, conv_b, ml_g, cos2, sin2, rt_g, batch, seq):
    t = batch * seq
    nchunk = seq // CHUNK
    qb, vb = ML_DQK, ML_DV
    assert (ML_HEADS, ML_DQK, ML_DV) == (RET_HEADS, RET_DQK, RET_DV)

    def pspec(width, off):
        return pl.BlockSpec((seq, width), lambda b, h, lg, off=off // width: (b, off + h))

    def gspec(row0):
        return pl.BlockSpec((1, nchunk, CHUNK), lambda b, h, lg, row0=row0: (row0 + h, b, 0))

    ml_specs = [pspec(qb, OFF_ML_Q), pspec(qb, OFF_ML_K), pspec(vb, OFF_ML_V), pspec(vb, OFF_ML_O),
                pspec(vb, OFF_ML_Z), gspec(0), gspec(F_ROW),
                pl.BlockSpec((ML_CONV, qb), lambda b, h, lg: (0, h)),
                pl.BlockSpec((ML_CONV, qb), lambda b, h, lg: (0, ML_HEADS + h)),
                pl.BlockSpec((1, qb), lambda b, h, lg: (0, h)),
                pl.BlockSpec((1, qb), lambda b, h, lg: (0, ML_HEADS + h)),
                pl.BlockSpec((1, vb), lambda b, h, lg: (0, h)),
                pl.BlockSpec(((ML_CONV - 1) * CONV_BLOCK, 2 * CONV_BLOCK), lambda b, h, lg: (0, 0))]
    rt_specs = [pspec(qb, OFF_RT_Q), pspec(qb, OFF_RT_K), pspec(vb, OFF_RT_V), pspec(vb, OFF_RT_Z),
                pl.BlockSpec((seq, qb), lambda b, h, lg: (b, 0)),
                pl.BlockSpec((seq, qb), lambda b, h, lg: (b, 0)),
                pl.BlockSpec((1, vb), lambda b, h, lg: (0, h))]
    assert len(ml_specs) == N_ML_IN and len(rt_specs) == N_RT_IN
    ml_scratch = [pltpu.VMEM((seq + CONV_BLOCK, 2 * qb), BF16),
                  pltpu.VMEM((seq, qb), F32),
                  pltpu.VMEM((seq, qb), F32),
                  pltpu.VMEM((nchunk, CHUNK), F32),
                  pltpu.VMEM((nchunk, CHUNK), F32),
                  pltpu.VMEM((nchunk, CHUNK), F32),
                  pltpu.VMEM((nchunk, CHUNK), F32),
                  pltpu.VMEM((nchunk, 1), F32),
                  pltpu.VMEM((nchunk, 1), F32),
                  pltpu.VMEM((nchunk, 1), F32),
                  pltpu.VMEM((qb, vb + LANES), F32)]
    assert len(ml_scratch) == N_ML_SCRATCH
    out_spec = pl.BlockSpec((seq, vb), lambda b, h, lg: (b, h))
    grid_spec = pltpu.PrefetchScalarGridSpec(
        num_scalar_prefetch=1,
        grid=(batch, ML_HEADS),
        in_specs=ml_specs + rt_specs,
        out_specs=[out_spec, out_spec],
        scratch_shapes=ml_scratch + [pltpu.VMEM((qb, vb), F32)],
    )
    return pl.pallas_call(
        functools.partial(_mixers_kernel, seq=seq),
        grid_spec=grid_spec,
        out_shape=[jax.ShapeDtypeStruct((t, ML_V), BF16), jax.ShapeDtypeStruct((t, RET_V), BF16)],
        compiler_params=_params(("arbitrary", "arbitrary")),
        name="mixers",
    )(log_gamma, p, p, p, p, p, gates3, gates3, conv_w, conv_w, conv_b, conv_b, ml_g,
      jnp.asarray(_conv_shift_matrix(), BF16), p, p, p, p, cos2, sin2, rt_g)


def _xattn_kernel(q_ref, z_ref, mk_ref, mv_ref, out_ref, *, blk):
    mk = mk_ref[...]
    mv = mv_ref[...]

    def block(i, carry):
        rows = pl.ds(pl.multiple_of(i * blk, blk), blk)
        sc = lax.dot_general(q_ref[rows, :], mk, (((1,), (1,)), ((), ())), preferred_element_type=F32)
        sc = sc * (XA_DH ** -0.5)
        e = jnp.exp(sc - jnp.max(sc, axis=-1, keepdims=True))
        o = _dot(e.astype(BF16), mv) * (1.0 / jnp.sum(e, axis=-1, keepdims=True))
        zh = 0.5 * z_ref[rows, :].astype(F32)
        out_ref[rows, :] = (o * (zh * (1.0 + jnp.tanh(zh)))).astype(BF16)
        return carry

    lax.fori_loop(0, q_ref.shape[0] // blk, block, 0, unroll=2)


def _xattn(p, memkv, batch, seq):
    t = batch * seq
    ts = seq
    ns = seq // ts
    w = XA_DH
    return pl.pallas_call(
        functools.partial(_xattn_kernel, blk=min(XATTN_BLOCK, ts)),
        grid=(batch, XA_HEADS, ns),
        in_specs=[pl.BlockSpec((ts, w), lambda b, h, s: (b * ns + s, OFF_XA_Q // w + h)),
                  pl.BlockSpec((ts, w), lambda b, h, s: (b * ns + s, OFF_XA_Z // w + h)),
                  pl.BlockSpec((MEM_TOKENS, w), lambda b, h, s: (b, h)),
                  pl.BlockSpec((MEM_TOKENS, w), lambda b, h, s: (b, XA_HEADS + h))],
        out_specs=pl.BlockSpec((ts, w), lambda b, h, s: (b * ns + s, h)),
        out_shape=jax.ShapeDtypeStruct((t, XA_W), BF16),
        compiler_params=_params(("arbitrary", "arbitrary", "arbitrary")),
        name="xattn",
    )(p, p, memkv, memkv)


def _merge_kernel(ml_ref, rt_ref, xa_ref, wml_ref, wrt_ref, wxa_ref, *refs):
    gate_refs, o_ref = refs[:-1], refs[-1]
    pieces = len(gate_refs) // N_BRANCH
    acc = None
    for br, (a_ref, w_ref) in enumerate(((ml_ref, wml_ref), (rt_ref, wrt_ref), (xa_ref, wxa_ref))):
        gate = jnp.concatenate([g[...] for g in gate_refs[br * pieces:(br + 1) * pieces]], axis=1)
        term = _sigmoid(gate.astype(F32)) * _dot(a_ref[...], w_ref[...])
        acc = term if acc is None else acc + term
    o_ref[...] = acc.astype(BF16)


def _merge(ml_out, rt_out, xa_out, w_ml, w_rt, w_xa, p, tm, tn):
    t = ml_out.shape[0]
    nj = D_MODEL // tn
    tg = MERGE_GATE_COLS
    assert OFF_GATE % tg == 0 and D_MODEL % tg == 0 and tn % tg == 0
    pieces = tn // tg

    def gspec(br, piece):
        first = (OFF_GATE + br * D_MODEL) // tg + piece
        return pl.BlockSpec((tm, tg), lambda i, j: (i, first + j * pieces))

    return pl.pallas_call(
        _merge_kernel,
        grid=(t // tm, nj),
        in_specs=[pl.BlockSpec((tm, ML_V), lambda i, j: (i, 0)),
                  pl.BlockSpec((tm, RET_V), lambda i, j: (i, 0)),
                  pl.BlockSpec((tm, XA_W), lambda i, j: (i, 0)),
                  pl.BlockSpec((ML_V, tn), lambda i, j: (0, j)),
                  pl.BlockSpec((RET_V, tn), lambda i, j: (0, j)),
                  pl.BlockSpec((XA_W, tn), lambda i, j: (0, j))]
        + [gspec(br, piece) for br in range(N_BRANCH) for piece in range(pieces)],
        out_specs=pl.BlockSpec((tm, tn), lambda i, j: (i, j)),
        out_shape=jax.ShapeDtypeStruct((t, D_MODEL), BF16),
        compiler_params=_params(("arbitrary", "arbitrary")),
        name="merge",
    )(ml_out, rt_out, xa_out, w_ml, w_rt, w_xa, *([p] * (N_BRANCH * pieces)))


def _outproj_kernel(x_ref, m_ref, w_ref, g_ref, o_ref):
    y = x_ref[...] + _dot(m_ref[...], w_ref[...])
    o_ref[...] = _rms(y, g_ref[...])


def _outproj(x2, merged, w_out, final_g, tm):
    t = x2.shape[0]
    return pl.pallas_call(
        _outproj_kernel,
        grid=(t // tm,),
        in_specs=[pl.BlockSpec((tm, D_MODEL), lambda i: (i, 0)),
                  pl.BlockSpec((tm, D_MODEL), lambda i: (i, 0)),
                  pl.BlockSpec((D_MODEL, D_MODEL), lambda i: (0, 0)),
                  pl.BlockSpec((1, D_MODEL), lambda i: (0, 0))],
        out_specs=pl.BlockSpec((tm, D_MODEL), lambda i: (i, 0)),
        out_shape=jax.ShapeDtypeStruct((t, D_MODEL), F32),
        compiler_params=_params(("arbitrary",)),
        name="outproj",
    )(x2, merged, w_out, final_g)


def kernel(x, mem, positions, ln_g, mem_ln_g, w_in, b_in, conv_w, conv_b, ml_hnorm_g, ret_hnorm_g, w_mem_kv,
           w_br_ml, w_br_ret, w_br_xa, w_out, final_g):
    batch, seq, _ = x.shape
    assert seq % CHUNK == 0 and ln_g.shape[0] == 1
    t = batch * seq
    x2 = x.reshape(t, D_MODEL)

    wt, b0 = w_in[0].T, b_in[0]
    w_full = wt.astype(BF16)
    w_main = lax.dynamic_update_slice(w_full, w_full[OFF_IF + 2 * ML_HEADS:], (OFF_IF, 0))
    b_main = jnp.concatenate([b0[:OFF_IF], b0[OFF_IF + 2 * ML_HEADS:]])[None, :]
    wif_t = jnp.zeros((GATE_ROWS, D_MODEL), F32)
    wif_t = wif_t.at[0:ML_HEADS].set(wt[OFF_IF:OFF_IF + ML_HEADS])
    wif_t = wif_t.at[F_ROW:F_ROW + ML_HEADS].set(wt[OFF_IF + ML_HEADS:OFF_IF + 2 * ML_HEADS]).astype(BF16)
    bif_t = jnp.zeros((GATE_ROWS, 1), F32)
    bif_t = bif_t.at[0:ML_HEADS, 0].set(b0[OFF_IF:OFF_IF + ML_HEADS])
    bif_t = bif_t.at[F_ROW:F_ROW + ML_HEADS, 0].set(b0[OFF_IF + ML_HEADS:OFF_IF + 2 * ML_HEADS])

    memkv = _memkv(mem.reshape(batch * MEM_TOKENS, D_MODEL), mem_ln_g[0][None, :], w_mem_kv[0].astype(BF16))

    p, gates_t = _inproj(x2, ln_g[0][None, :], w_main, b_main, wif_t, bif_t, min(INPROJ_ROWS, t), INPROJ_COLS,
                         INPROJ_NORM_STEPS)
    gates3 = gates_t.reshape(GATE_ROWS, t // CHUNK, CHUNK)

    half = RET_DQK // 2
    freqs = ROPE_BASE ** (-jnp.arange(half, dtype=F32) / half)
    freq2 = jnp.concatenate([freqs, freqs])[None, :]
    cos2, sin2 = _rope_tables(positions.reshape(2, t // 2).T, freq2)
    log_gamma = jnp.asarray(np.log(1.0 - 2.0 ** (-5.0 - np.arange(RET_HEADS))), dtype=F32)

    ml_out, rt_out = _mixers(log_gamma, p, gates3, conv_w[0], conv_b[0][None, :], ml_hnorm_g[0][None, :],
                             cos2, sin2, ret_hnorm_g[0][None, :], batch, seq)
    xa_out = _xattn(p, memkv, batch, seq)

    merged = _merge(ml_out, rt_out, xa_out, w_br_ml[0].astype(BF16), w_br_ret[0].astype(BF16),
                    w_br_xa[0].astype(BF16), p, min(MERGE_ROWS, t), MERGE_COLS)
    out = _outproj(x2, merged, w_out[0].astype(BF16), final_g[None, :], min(OUTPROJ_ROWS, t))
    return out.reshape(batch, seq, D_MODEL)
```

```python
import functools

import numpy as np
import jax
import jax.numpy as jnp
from jax import lax
from jax.experimental import pallas as pl
from jax.experimental.pallas import tpu as pltpu

F32 = jnp.float32
BF16 = jnp.bfloat16

D_MODEL = 2048
MEM_TOKENS = 256
ML_HEADS = 6
ML_DQK = 128
ML_DV = 256
ML_CONV = 4
RET_HEADS = 6
RET_DQK = 128
RET_DV = 256
XA_HEADS = 4
XA_DH = 256
ROPE_BASE = 10000.0
EPS = 1e-6
N_BRANCH = 3

ML_QK = ML_HEADS * ML_DQK
ML_V = ML_HEADS * ML_DV
RET_QK = RET_HEADS * RET_DQK
RET_V = RET_HEADS * RET_DV
XA_W = XA_HEADS * XA_DH

OFF_ML_Q = 0
OFF_ML_K = OFF_ML_Q + ML_QK
OFF_ML_V = OFF_ML_K + ML_QK
OFF_ML_O = OFF_ML_V + ML_V
OFF_ML_Z = OFF_ML_O + ML_V
OFF_IF = OFF_ML_Z + ML_V
OFF_RT_Q = OFF_ML_Z + ML_V
OFF_RT_K = OFF_RT_Q + RET_QK
OFF_RT_V = OFF_RT_K + RET_QK
OFF_RT_Z = OFF_RT_V + RET_V
OFF_XA_Q = OFF_RT_Z + RET_V
OFF_XA_Z = OFF_XA_Q + XA_W
OFF_GATE = OFF_XA_Z + XA_W
N_MAIN = OFF_GATE + N_BRANCH * D_MODEL

CHUNK = 256
GATE_ROWS = 16
F_ROW = 8
CONV_BLOCK = 128

LANES = 128
VMEM_LIMIT = 58 * 1024 * 1024

INPROJ_ROWS, INPROJ_COLS = 2048, 1280
INPROJ_NORM_STEPS = 2
MERGE_ROWS, MERGE_COLS = 1024, 1024
MERGE_GATE_COLS = 512
OUTPROJ_ROWS = 512
MEMKV_ROWS = 512
ROPE_ROWS = 2048
XATTN_BLOCK = 1024


def _params(sem):
    return pltpu.CompilerParams(dimension_semantics=sem, vmem_limit_bytes=VMEM_LIMIT)


def _sigmoid(x):
    return 0.5 * jnp.tanh(0.5 * x) + 0.5


def _rms(xf, g):
    return xf * lax.rsqrt(jnp.mean(xf * xf, axis=-1, keepdims=True) + EPS) * g


def _head_norm(t, g):
    mu = jnp.mean(t, axis=-1, keepdims=True)
    c = t - mu
    var = jnp.mean(c * c, axis=-1, keepdims=True)
    return c * lax.rsqrt(var + EPS) * g


def _dot(a, b):
    return jnp.dot(a, b, preferred_element_type=F32)


def _memkv_kernel(m_ref, g_ref, w_ref, o_ref):
    h = _rms(m_ref[...], g_ref[...])
    o_ref[...] = _dot(h.astype(BF16), w_ref[...]).astype(BF16)


def _memkv(mem2, g, w):
    rows = mem2.shape[0]
    tm = min(MEMKV_ROWS, rows)
    return pl.pallas_call(
        _memkv_kernel,
        grid=(rows // tm,),
        in_specs=[pl.BlockSpec((tm, D_MODEL), lambda i: (i, 0)),
                  pl.BlockSpec((1, D_MODEL), lambda i: (0, 0)),
                  pl.BlockSpec((D_MODEL, 2 * XA_W), lambda i: (0, 0))],
        out_specs=pl.BlockSpec((tm, 2 * XA_W), lambda i: (i, 0)),
        out_shape=jax.ShapeDtypeStruct((rows, 2 * XA_W), BF16),
        compiler_params=_params(("arbitrary",)),
        name="memkv",
    )(mem2, g, w)


def _inproj_kernel(x_ref, g_ref, w_ref, b_ref, wif_ref, bif_ref, p_ref, gate_ref, hb_ref, *, nsplit, tail):
    j = pl.program_id(1)
    last = pl.num_programs(1) - 1
    tx = x_ref.shape[0]
    tn = w_ref.shape[0]

    def project(width):
        acc = lax.dot_general(hb_ref[...], w_ref[pl.ds(0, width), :], (((1,), (1,)), ((), ())),
                              preferred_element_type=F32)
        p_ref[:, pl.ds(0, width)] = (acc + b_ref[:, pl.ds(0, width)]).astype(BF16)

    @pl.when(j < nsplit)
    def _():
        hb = _rms(x_ref[...], g_ref[...]).astype(BF16)
        r0 = pl.multiple_of(j * tx, tx)
        hb_ref[pl.ds(r0, tx), :] = hb
        gt = lax.dot_general(wif_ref[...], hb, (((1,), (1,)), ((), ())), preferred_element_type=F32)
        gate_ref[:, pl.ds(r0, tx)] = gt + bif_ref[...]

    if tail == tn:
        pl.when(j >= nsplit)(lambda: project(tn))
    else:
        pl.when(jnp.logical_and(j >= nsplit, j < last))(lambda: project(tn))
        pl.when(j == last)(lambda: project(tail))


def _inproj(x2, g, w_main, b_main, wif_t, bif_t, tm, tn, nsplit):
    t = x2.shape[0]
    tx = tm // nsplit
    ncol = pl.cdiv(N_MAIN, tn)
    tail = N_MAIN - (ncol - 1) * tn

    def col(j):
        return jnp.maximum(j - nsplit, 0)

    return pl.pallas_call(
        functools.partial(_inproj_kernel, nsplit=nsplit, tail=tail),
        grid=(t // tm, nsplit + ncol),
        in_specs=[pl.BlockSpec((tx, D_MODEL), lambda i, j: (i * nsplit + jnp.minimum(j, nsplit - 1), 0)),
                  pl.BlockSpec((1, D_MODEL), lambda i, j: (0, 0)),
                  pl.BlockSpec((tn, D_MODEL), lambda i, j: (col(j), 0)),
                  pl.BlockSpec((1, tn), lambda i, j: (0, col(j))),
                  pl.BlockSpec((GATE_ROWS, D_MODEL), lambda i, j: (0, 0)),
                  pl.BlockSpec((GATE_ROWS, 1), lambda i, j: (0, 0))],
        out_specs=[pl.BlockSpec((tm, tn), lambda i, j: (i, col(j))),
                   pl.BlockSpec((GATE_ROWS, tm), lambda i, j: (0, i))],
        out_shape=[jax.ShapeDtypeStruct((t, N_MAIN), BF16),
                   jax.ShapeDtypeStruct((GATE_ROWS, t), F32)],
        scratch_shapes=[pltpu.VMEM((tm, D_MODEL), BF16)],
        compiler_params=_params(("arbitrary", "arbitrary")),
        name="inproj",
    )(x2, g, w_main, b_main, wif_t, bif_t)


def _rope_kernel(pos_ref, freq_ref, cos_ref, sin_ref):
    half = RET_DQK // 2
    low = lax.broadcasted_iota(jnp.int32, (1, RET_DQK), 1) < half
    pos = jnp.where(low, pos_ref[:, 0:1], pos_ref[:, 1:2]).astype(F32)
    ang = pos * freq_ref[...]
    c = jnp.cos(ang)
    s = jnp.sin(ang)
    c_other = pltpu.roll(c, half, 1)
    s_other = pltpu.roll(s, half, 1)
    cos_ref[0] = jnp.where(low, c, c_other)
    cos_ref[1] = jnp.where(low, c_other, c)
    sin_ref[0] = jnp.where(low, -s, s_other)
    sin_ref[1] = jnp.where(low, -s_other, s)


def _rope_tables(pos_pair, freq2):
    th = pos_pair.shape[0]
    tm = min(ROPE_ROWS, th)
    out_spec = pl.BlockSpec((2, tm, RET_DQK), lambda i: (0, i, 0))
    out_shape = jax.ShapeDtypeStruct((2, th, RET_DQK), F32)
    cos2, sin2 = pl.pallas_call(
        _rope_kernel,
        grid=(th // tm,),
        in_specs=[pl.BlockSpec((tm, 2), lambda i: (i, 0)),
                  pl.BlockSpec((1, RET_DQK), lambda i: (0, 0))],
        out_specs=[out_spec, out_spec],
        out_shape=[out_shape, out_shape],
        compiler_params=_params(("arbitrary",)),
        name="rope",
    )(pos_pair, freq2)
    return cos2.reshape(2 * th, RET_DQK), sin2.reshape(2 * th, RET_DQK)


def _split3(x):
    hi = x.astype(BF16)
    r1 = x - hi.astype(F32)
    mid = r1.astype(BF16)
    lo = (r1 - mid.astype(F32)).astype(BF16)
    return hi, mid, lo


def _conv_shift_matrix():
    s = np.zeros(((ML_CONV - 1) * CONV_BLOCK, 2 * CONV_BLOCK), np.float32)
    t = np.arange(CONV_BLOCK)
    for j in range(ML_CONV - 1):
        s[j * CONV_BLOCK + t, CONV_BLOCK + t - (ML_CONV - 1) + j] = 1.0
    return s


def _mlstm_setup(q_ref, k_ref, v_ref, o_ref, z_ref, gi_ref, gf_ref, cwq_ref, cwk_ref, cbq_ref, cbk_ref,
                 hg_ref, shift_ref, out_ref, pad_ref, qc_ref, kc_ref, u_ref, mr_ref, a_ref, w_ref, mp_ref, mn_ref,
                 dec_ref, caug_ref, *, seq):
    L = CHUNK
    nchunk = seq // L

    pad_ref[pl.ds(0, CONV_BLOCK), :] = jnp.zeros((CONV_BLOCK, 2 * ML_DQK), BF16)
    pad_ref[pl.ds(CONV_BLOCK, seq), pl.ds(0, ML_DQK)] = q_ref[...]
    pad_ref[pl.ds(CONV_BLOCK, seq), pl.ds(ML_DQK, ML_DQK)] = k_ref[...]
    cw = jnp.concatenate([cwq_ref[...], cwk_ref[...]], axis=-1)
    cb = jnp.concatenate([cbq_ref[...], cbk_ref[...]], axis=-1)
    shift = shift_ref[...]

    def conv_block(i, carry):
        r = pl.multiple_of(i * CONV_BLOCK, CONV_BLOCK)
        taps = _dot(shift, pad_ref[pl.ds(r, 2 * CONV_BLOCK), :])
        y = cb + pad_ref[pl.ds(r + CONV_BLOCK, CONV_BLOCK), :].astype(F32) * cw[ML_CONV - 1:ML_CONV, :]
        for j in range(ML_CONV - 1):
            y = y + taps[j * CONV_BLOCK:(j + 1) * CONV_BLOCK, :] * cw[j:j + 1, :]
        yh = 0.5 * y
        y = yh * (1.0 + jnp.tanh(yh))
        qc_ref[pl.ds(r, CONV_BLOCK), :] = y[:, :ML_DQK] * (ML_DQK ** -0.5)
        kc_ref[pl.ds(r, CONV_BLOCK), :] = y[:, ML_DQK:]
        return carry

    lax.fori_loop(0, seq // CONV_BLOCK, conv_block, 0, unroll=16)

    f = gf_ref[0]
    lf = jnp.minimum(f, 0.0) - jnp.log1p(jnp.exp(-jnp.abs(f)))
    row = lax.broadcasted_iota(jnp.int32, (L, L), 0)
    col = lax.broadcasted_iota(jnp.int32, (L, L), 1)
    triu = (row <= col).astype(BF16)
    hi, mid, lo = _split3(lf)
    bb = _dot(hi, triu) + _dot(mid, triu) + _dot(lo, triu)
    causal = row >= col
    diag = row == col

    gi = gi_ref[0]
    g_col = bb[:, L - 1:L]
    log_w = g_col - bb + gi
    w_max = jnp.max(log_w, axis=-1, keepdims=True)
    m = jnp.zeros((1, 1), F32)
    for c in range(nchunk):
        mp_ref[c:c + 1, :] = m
        m = jnp.maximum(g_col[c:c + 1, :] + m, w_max[c:c + 1, :])
        mn_ref[c:c + 1, :] = m
    m_prev_all = mp_ref[...]
    m_new_all = mn_ref[...]
    a_all = gi - bb
    a_ref[...] = a_all
    w_ref[...] = jnp.exp(log_w - m_new_all)
    dec_ref[...] = jnp.exp(g_col + m_prev_all - m_new_all)
    lane = lax.broadcasted_iota(jnp.int32, a_all.shape, 1)
    run = a_all
    shift_by = 1
    while shift_by < L:
        run = jnp.maximum(run, jnp.where(lane >= shift_by, pltpu.roll(run, shift_by, 1), -jnp.inf))
        shift_by *= 2
    u_all = jnp.maximum(m_prev_all, run)
    u_ref[...] = u_all
    mr_ref[...] = bb + u_all
    eye3 = jnp.tile(diag.astype(BF16), (1, 3))

    caug_ref[...] = jnp.zeros_like(caug_ref)
    hg_half = 0.5 * hg_ref[...]
    ones_blk = jnp.ones((L, LANES), BF16)

    def chunk(c):
        r0 = pl.multiple_of(c * L, L)
        rows = pl.ds(r0, L)
        one = pl.ds(c, 1)
        a_row = a_ref[one, :]
        w_row = w_ref[one, :]
        m_prev = mp_ref[one, :]
        decay = dec_ref[one, :]
        u_pieces = jnp.concatenate(_split3(u_ref[one, :]), axis=1)
        m_pieces = jnp.concatenate(_split3(mr_ref[one, :]), axis=1)
        rows_t = jnp.concatenate([jnp.broadcast_to(u_pieces, (LANES, 3 * L)),
                                  jnp.broadcast_to(m_pieces, (LANES, 3 * L))], axis=0)
        um = lax.dot_general(eye3, rows_t, (((1,), (1,)), ((), ())), preferred_element_type=F32)
        u_rep = um[:, :LANES]
        d = jnp.exp(jnp.where(causal, a_row, -jnp.inf) - jnp.tile(u_rep, (1, L // LANES)))
        inter = jnp.exp(m_prev - u_rep)
        qf = qc_ref[rows, :]
        kf = kc_ref[rows, :]
        qb = qf.astype(BF16)
        kt = kf.T
        vb = v_ref[rows, :]
        s = _dot(qb, kt.astype(BF16)) * d
        caug = caug_ref[...]
        vaug = jnp.concatenate([vb, ones_blk], axis=1)
        na = _dot(s.astype(BF16), vaug) + jnp.tile(inter, (1, ML_DV // LANES + 1)) * _dot(qb, caug.astype(BF16))
        den = jnp.maximum(jnp.abs(na[:, ML_DV:]), jnp.exp(-um[:, LANES:]))

        caug_ref[...] = decay * caug + _dot((kt * w_row).astype(BF16), vaug)

        num = na[:, :ML_DV]
        cen = num - jnp.mean(num, axis=-1, keepdims=True)
        var = jnp.mean(cen * cen, axis=-1, keepdims=True)
        y = cen * jnp.tile(lax.rsqrt(var + EPS * (den * den)), (1, ML_DV // LANES)) * hg_half
        oh = 0.5 * o_ref[rows, :].astype(F32)
        zh = 0.5 * z_ref[rows, :].astype(F32)
        y = y * (1.0 + jnp.tanh(oh)) * (zh * (1.0 + jnp.tanh(zh)))
        out_ref[rows, :] = y.astype(BF16)

    return chunk


def _ret_setup(lg_ref, q_ref, k_ref, v_ref, z_ref, cos_ref, sin_ref, hg_ref, out_ref, r_ref):
    L = CHUNK
    lg = lg_ref[pl.program_id(1)]
    row = lax.broadcasted_iota(jnp.int32, (L, L), 0)
    col = lax.broadcasted_iota(jnp.int32, (L, L), 1)
    intra = jnp.where(row >= col, jnp.exp((row - col).astype(F32) * lg), 0.0)
    pos_col = lax.broadcasted_iota(jnp.int32, (L, 1), 0).astype(F32)
    pos_row = lax.broadcasted_iota(jnp.int32, (1, L), 1).astype(F32)
    q_decay = jnp.exp((pos_col + 1.0) * lg)
    k_decay = jnp.exp((L - 1.0 - pos_row) * lg)
    chunk_decay = jnp.exp(jnp.full((1, 1), float(L), F32) * lg)
    r_ref[...] = jnp.zeros_like(r_ref)
    half = RET_DQK // 2
    hg = hg_ref[...]

    def chunk(c):
        r0 = pl.multiple_of(c * L, L)
        rows = pl.ds(r0, L)
        cs = cos_ref[rows, :]
        sn = sin_ref[rows, :]
        qf = q_ref[rows, :].astype(F32)
        kf = k_ref[rows, :].astype(F32)
        qr = qf * cs + pltpu.roll(qf, half, 1) * sn
        kr = (kf * cs + pltpu.roll(kf, half, 1) * sn) * (RET_DQK ** -0.5)
        qb = qr.astype(BF16)
        kt = kr.T
        vb = v_ref[rows, :]
        s = _dot(qb, kt.astype(BF16)) * intra
        r_state = r_ref[...]
        o = _dot(s.astype(BF16), vb) + q_decay * _dot(qb, r_state.astype(BF16))
        r_ref[...] = chunk_decay * r_state + _dot((kt * k_decay).astype(BF16), vb)
        y = _head_norm(o, hg)
        zh = 0.5 * z_ref[rows, :].astype(F32)
        out_ref[rows, :] = (y * (zh * (1.0 + jnp.tanh(zh)))).astype(BF16)

    return chunk


N_ML_IN = 13
N_RT_IN = 7
N_ML_SCRATCH = 11
CHUNK_UNROLL = 4


def _mixers_kernel(lg_ref, *refs, seq):
    ml_in = refs[:N_ML_IN]
    rt_in = refs[N_ML_IN:N_ML_IN + N_RT_IN]
    ml_out, rt_out = refs[N_ML_IN + N_RT_IN:N_ML_IN + N_RT_IN + 2]
    scratch = refs[N_ML_IN + N_RT_IN + 2:]
    nchunk = seq // CHUNK
    ml_chunk = _mlstm_setup(*ml_in, ml_out, *scratch[:N_ML_SCRATCH], seq=seq)
    lax.fori_loop(0, nchunk, lambda c, carry: (ml_chunk(c), carry)[1], 0, unroll=CHUNK_UNROLL)
    rt_chunk = _ret_setup(lg_ref, *rt_in, rt_out, scratch[N_ML_SCRATCH])
    lax.fori_loop(0, nchunk, lambda c, carry: (rt_chunk(c), carry)[1], 0, unroll=CHUNK_UNROLL)


def _mixers(log_gamma, p, gates3, conv_w, conv_b, ml_g, cos2, sin2, rt_g, batch, seq):
    t = batch * seq
    nchunk = seq // CHUNK
    qb, vb = ML_DQK, ML_DV
    assert (ML_HEADS, ML_DQK, ML_DV) == (RET_HEADS, RET_DQK, RET_DV)

    def pspec(width, off):
        return pl.BlockSpec((seq, width), lambda b, h, lg, off=off // width: (b, off + h))

    def gspec(row0):
        return pl.BlockSpec((1, nchunk, CHUNK), lambda b, h, lg, row0=row0: (row0 + h, b, 0))

    ml_specs = [pspec(qb, OFF_ML_Q), pspec(qb, OFF_ML_K), pspec(vb, OFF_ML_V), pspec(vb, OFF_ML_O),
                pspec(vb, OFF_ML_Z), gspec(0), gspec(F_ROW),
                pl.BlockSpec((ML_CONV, qb), lambda b, h, lg: (0, h)),
                pl.BlockSpec((ML_CONV, qb), lambda b, h, lg: (0, ML_HEADS + h)),
                pl.BlockSpec((1, qb), lambda b, h, lg: (0, h)),
                pl.BlockSpec((1, qb), lambda b, h, lg: (0, ML_HEADS + h)),
                pl.BlockSpec((1, vb), lambda b, h, lg: (0, h)),
                pl.BlockSpec(((ML_CONV - 1) * CONV_BLOCK, 2 * CONV_BLOCK), lambda b, h, lg: (0, 0))]
    rt_specs = [pspec(qb, OFF_RT_Q), pspec(qb, OFF_RT_K), pspec(vb, OFF_RT_V), pspec(vb, OFF_RT_Z),
                pl.BlockSpec((seq, qb), lambda b, h, lg: (b, 0)),
                pl.BlockSpec((seq, qb), lambda b, h, lg: (b, 0)),
                pl.BlockSpec((1, vb), lambda b, h, lg: (0, h))]
    assert len(ml_specs) == N_ML_IN and len(rt_specs) == N_RT_IN
    ml_scratch = [pltpu.VMEM((seq + CONV_BLOCK, 2 * qb), BF16),
                  pltpu.VMEM((seq, qb), F32),
                  pltpu.VMEM((seq, qb), F32),
                  pltpu.VMEM((nchunk, CHUNK), F32),
                  pltpu.VMEM((nchunk, CHUNK), F32),
                  pltpu.VMEM((nchunk, CHUNK), F32),
                  pltpu.VMEM((nchunk, CHUNK), F32),
                  pltpu.VMEM((nchunk, 1), F32),
                  pltpu.VMEM((nchunk, 1), F32),
                  pltpu.VMEM((nchunk, 1), F32),
                  pltpu.VMEM((qb, vb + LANES), F32)]
    assert len(ml_scratch) == N_ML_SCRATCH
    out_spec = pl.BlockSpec((seq, vb), lambda b, h, lg: (b, h))
    grid_spec = pltpu.PrefetchScalarGridSpec(
        num_scalar_prefetch=1,
        grid=(batch, ML_HEADS),
        in_specs=ml_specs + rt_specs,
        out_specs=[out_spec, out_spec],
        scratch_shapes=ml_scratch + [pltpu.VMEM((qb, vb), F32)],
    )
    return pl.pallas_call(
        functools.partial(_mixers_kernel, seq=seq),
        grid_spec=grid_spec,
        out_shape=[jax.ShapeDtypeStruct((t, ML_V), BF16), jax.ShapeDtypeStruct((t, RET_V), BF16)],
        compiler_params=_params(("arbitrary", "arbitrary")),
        name="mixers",
    )(log_gamma, p, p, p, p, p, gates3, gates3, conv_w, conv_w, conv_b, conv_b, ml_g,
      jnp.asarray(_conv_shift_matrix(), BF16), p, p, p, p, cos2, sin2, rt_g)


def _xattn_kernel(q_ref, z_ref, mk_ref, mv_ref, out_ref, *, blk):
    mk = mk_ref[...]
    mv = mv_ref[...]

    def block(i, carry):
        rows = pl.ds(pl.multiple_of(i * blk, blk), blk)
        sc = lax.dot_general(q_ref[rows, :], mk, (((1,), (1,)), ((), ())), preferred_element_type=F32)
        sc = sc * (XA_DH ** -0.5)
        e = jnp.exp(sc - jnp.max(sc, axis=-1, keepdims=True))
        o = _dot(e.astype(BF16), mv) * (1.0 / jnp.sum(e, axis=-1, keepdims=True))
        zh = 0.5 * z_ref[rows, :].astype(F32)
        out_ref[rows, :] = (o * (zh * (1.0 + jnp.tanh(zh)))).astype(BF16)
        return carry

    lax.fori_loop(0, q_ref.shape[0] // blk, block, 0, unroll=2)


def _xattn(p, memkv, batch, seq):
    t = batch * seq
    ts = seq
    ns = seq // ts
    w = XA_DH
    return pl.pallas_call(
        functools.partial(_xattn_kernel, blk=min(XATTN_BLOCK, ts)),
        grid=(batch, XA_HEADS, ns),
        in_specs=[pl.BlockSpec((ts, w), lambda b, h, s: (b * ns + s, OFF_XA_Q // w + h)),
                  pl.BlockSpec((ts, w), lambda b, h, s: (b * ns + s, OFF_XA_Z // w + h)),
                  pl.BlockSpec((MEM_TOKENS, w), lambda b, h, s: (b, h)),
                  pl.BlockSpec((MEM_TOKENS, w), lambda b, h, s: (b, XA_HEADS + h))],
        out_specs=pl.BlockSpec((ts, w), lambda b, h, s: (b * ns + s, h)),
        out_shape=jax.ShapeDtypeStruct((t, XA_W), BF16),
        compiler_params=_params(("arbitrary", "arbitrary", "arbitrary")),
        name="xattn",
    )(p, p, memkv, memkv)


def _merge_kernel(ml_ref, rt_ref, xa_ref, wml_ref, wrt_ref, wxa_ref, *refs):
    gate_refs, o_ref = refs[:-1], refs[-1]
    pieces = len(gate_refs) // N_BRANCH
    acc = None
    for br, (a_ref, w_ref) in enumerate(((ml_ref, wml_ref), (rt_ref, wrt_ref), (xa_ref, wxa_ref))):
        gate = jnp.concatenate([g[...] for g in gate_refs[br * pieces:(br + 1) * pieces]], axis=1)
        term = _sigmoid(gate.astype(F32)) * _dot(a_ref[...], w_ref[...])
        acc = term if acc is None else acc + term
    o_ref[...] = acc.astype(BF16)


def _merge(ml_out, rt_out, xa_out, w_ml, w_rt, w_xa, p, tm, tn):
    t = ml_out.shape[0]
    nj = D_MODEL // tn
    tg = MERGE_GATE_COLS
    assert OFF_GATE % tg == 0 and D_MODEL % tg == 0 and tn % tg == 0
    pieces = tn // tg

    def gspec(br, piece):
        first = (OFF_GATE + br * D_MODEL) // tg + piece
        return pl.BlockSpec((tm, tg), lambda i, j: (i, first + j * pieces))

    return pl.pallas_call(
        _merge_kernel,
        grid=(t // tm, nj),
        in_specs=[pl.BlockSpec((tm, ML_V), lambda i, j: (i, 0)),
                  pl.BlockSpec((tm, RET_V), lambda i, j: (i, 0)),
                  pl.BlockSpec((tm, XA_W), lambda i, j: (i, 0)),
                  pl.BlockSpec((ML_V, tn), lambda i, j: (0, j)),
                  pl.BlockSpec((RET_V, tn), lambda i, j: (0, j)),
                  pl.BlockSpec((XA_W, tn), lambda i, j: (0, j))]
        + [gspec(br, piece) for br in range(N_BRANCH) for piece in range(pieces)],
        out_specs=pl.BlockSpec((tm, tn), lambda i, j: (i, j)),
        out_shape=jax.ShapeDtypeStruct((t, D_MODEL), BF16),
        compiler_params=_params(("arbitrary", "arbitrary")),
        name="merge",
    )(ml_out, rt_out, xa_out, w_ml, w_rt, w_xa, *([p] * (N_BRANCH * pieces)))


def _outproj_kernel(x_ref, m_ref, w_ref, g_ref, o_ref):
    y = x_ref[...] + _dot(m_ref[...], w_ref[...])
    o_ref[...] = _rms(y, g_ref[...])


def _outproj(x2, merged, w_out, final_g, tm):
    t = x2.shape[0]
    return pl.pallas_call(
        _outproj_kernel,
        grid=(t // tm,),
        in_specs=[pl.BlockSpec((tm, D_MODEL), lambda i: (i, 0)),
                  pl.BlockSpec((tm, D_MODEL), lambda i: (i, 0)),
                  pl.BlockSpec((D_MODEL, D_MODEL), lambda i: (0, 0)),
                  pl.BlockSpec((1, D_MODEL), lambda i: (0, 0))],
        out_specs=pl.BlockSpec((tm, D_MODEL), lambda i: (i, 0)),
        out_shape=jax.ShapeDtypeStruct((t, D_MODEL), F32),
        compiler_params=_params(("arbitrary",)),
        name="outproj",
    )(x2, merged, w_out, final_g)


def kernel(x, mem, positions, ln_g, mem_ln_g, w_in, b_in, conv_w, conv_b, ml_hnorm_g, ret_hnorm_g, w_mem_kv,
           w_br_ml, w_br_ret, w_br_xa, w_out, final_g):
    batch, seq, _ = x.shape
    assert seq % CHUNK == 0 and ln_g.shape[0] == 1
    t = batch * seq
    x2 = x.reshape(t, D_MODEL)

    wt, b0 = w_in[0].T, b_in[0]
    w_full = wt.astype(BF16)
    w_main = lax.dynamic_update_slice(w_full, w_full[OFF_IF + 2 * ML_HEADS:], (OFF_IF, 0))
    b_main = jnp.concatenate([b0[:OFF_IF], b0[OFF_IF + 2 * ML_HEADS:]])[None, :]
    wif_t = jnp.zeros((GATE_ROWS, D_MODEL), F32)
    wif_t = wif_t.at[0:ML_HEADS].set(wt[OFF_IF:OFF_IF + ML_HEADS])
    wif_t = wif_t.at[F_ROW:F_ROW + ML_HEADS].set(wt[OFF_IF + ML_HEADS:OFF_IF + 2 * ML_HEADS]).astype(BF16)
    bif_t = jnp.zeros((GATE_ROWS, 1), F32)
    bif_t = bif_t.at[0:ML_HEADS, 0].set(b0[OFF_IF:OFF_IF + ML_HEADS])
    bif_t = bif_t.at[F_ROW:F_ROW + ML_HEADS, 0].set(b0[OFF_IF + ML_HEADS:OFF_IF + 2 * ML_HEADS])

    memkv = _memkv(mem.reshape(batch * MEM_TOKENS, D_MODEL), mem_ln_g[0][None, :], w_mem_kv[0].astype(BF16))

    p, gates_t = _inproj(x2, ln_g[0][None, :], w_main, b_main, wif_t, bif_t, min(INPROJ_ROWS, t), INPROJ_COLS,
                         INPROJ_NORM_STEPS)
    gates3 = gates_t.reshape(GATE_ROWS, t // CHUNK, CHUNK)

    half = RET_DQK // 2
    freqs = ROPE_BASE ** (-jnp.arange(half, dtype=F32) / half)
    freq2 = jnp.concatenate([freqs, freqs])[None, :]
    cos2, sin2 = _rope_tables(positions.reshape(2, t // 2).T, freq2)
    log_gamma = jnp.asarray(np.log(1.0 - 2.0 ** (-5.0 - np.arange(RET_HEADS))), dtype=F32)

    ml_out, rt_out = _mixers(log_gamma, p, gates3, conv_w[0], conv_b[0][None, :], ml_hnorm_g[0][None, :],
                             cos2, sin2, ret_hnorm_g[0][None, :], batch, seq)
    xa_out = _xattn(p, memkv, batch, seq)

    merged = _merge(ml_out, rt_out, xa_out, w_br_ml[0].astype(BF16), w_br_ret[0].astype(BF16),
                    w_br_xa[0].astype(BF16), p, min(MERGE_ROWS, t), MERGE_COLS)
    out = _outproj(x2, merged, w_out[0].astype(BF16), final_g[None, :], min(OUTPROJ_ROWS, t))
    return out.reshape(batch, seq, D_MODEL)
```

```python
import functools

import numpy as np
import jax
import jax.numpy as jnp
from jax import lax
from jax.experimental import pallas as pl
from jax.experimental.pallas import tpu as pltpu

F32 = jnp.float32
BF16 = jnp.bfloat16

D_MODEL = 2048
MEM_TOKENS = 256
ML_HEADS = 6
ML_DQK = 128
ML_DV = 256
ML_CONV = 4
RET_HEADS = 6
RET_DQK = 128
RET_DV = 256
XA_HEADS = 4
XA_DH = 256
ROPE_BASE = 10000.0
EPS = 1e-6
N_BRANCH = 3

ML_QK = ML_HEADS * ML_DQK
ML_V = ML_HEADS * ML_DV
RET_QK = RET_HEADS * RET_DQK
RET_V = RET_HEADS * RET_DV
XA_W = XA_HEADS * XA_DH

OFF_ML_Q = 0
OFF_ML_K = OFF_ML_Q + ML_QK
OFF_ML_V = OFF_ML_K + ML_QK
OFF_ML_O = OFF_ML_V + ML_V
OFF_ML_Z = OFF_ML_O + ML_V
OFF_IF = OFF_ML_Z + ML_V
OFF_RT_Q = OFF_ML_Z + ML_V
OFF_RT_K = OFF_RT_Q + RET_QK
OFF_RT_V = OFF_RT_K + RET_QK
OFF_RT_Z = OFF_RT_V + RET_V
OFF_XA_Q = OFF_RT_Z + RET_V
OFF_XA_Z = OFF_XA_Q + XA_W
OFF_GATE = OFF_XA_Z + XA_W
N_MAIN = OFF_GATE + N_BRANCH * D_MODEL

CHUNK = 256
GATE_ROWS = 16
F_ROW = 8
CONV_BLOCK = 128

LANES = 128
VMEM_LIMIT = 56 * 1024 * 1024

INPROJ_ROWS, INPROJ_COLS = 2048, 1024
INPROJ_NORM_STEPS = 2
MERGE_ROWS, MERGE_COLS = 1024, 1024
MERGE_GATE_COLS = 512
OUTPROJ_ROWS = 1024
OUTPROJ_VMEM_LIMIT = 60 * 1024 * 1024
MEMKV_ROWS = 512
ROPE_ROWS = 2048
XATTN_BLOCK = 1024


def _params(sem, vmem_limit=VMEM_LIMIT):
    return pltpu.CompilerParams(dimension_semantics=sem, vmem_limit_bytes=vmem_limit)


def _sigmoid(x):
    return 0.5 * jnp.tanh(0.5 * x) + 0.5


def _rms(xf, g):
    return xf * lax.rsqrt(jnp.mean(xf * xf, axis=-1, keepdims=True) + EPS) * g


def _head_norm(t, g):
    mu = jnp.mean(t, axis=-1, keepdims=True)
    c = t - mu
    var = jnp.mean(c * c, axis=-1, keepdims=True)
    return c * lax.rsqrt(var + EPS) * g


def _dot(a, b):
    return jnp.dot(a, b, preferred_element_type=F32)


def _memkv_kernel(m_ref, g_ref, w_ref, o_ref):
    h = _rms(m_ref[...], g_ref[...])
    o_ref[...] = _dot(h.astype(BF16), w_ref[...]).astype(BF16)


def _memkv(mem2, g, w):
    rows = mem2.shape[0]
    tm = min(MEMKV_ROWS, rows)
    return pl.pallas_call(
        _memkv_kernel,
        grid=(rows // tm,),
        in_specs=[pl.BlockSpec((tm, D_MODEL), lambda i: (i, 0)),
                  pl.BlockSpec((1, D_MODEL), lambda i: (0, 0)),
                  pl.BlockSpec((D_MODEL, 2 * XA_W), lambda i: (0, 0))],
        out_specs=pl.BlockSpec((tm, 2 * XA_W), lambda i: (i, 0)),
        out_shape=jax.ShapeDtypeStruct((rows, 2 * XA_W), BF16),
        compiler_params=_params(("arbitrary",)),
        name="memkv",
    )(mem2, g, w)


def _inproj_kernel(x_ref, g_ref, w_ref, b_ref, wif_ref, bif_ref, p_ref, gate_ref, hb_ref, *, nsplit, tail):
    j = pl.program_id(1)
    last = pl.num_programs(1) - 1
    tx = x_ref.shape[0]
    tn = w_ref.shape[0]

    def project(width):
        acc = lax.dot_general(hb_ref[...], w_ref[pl.ds(0, width), :], (((1,), (1,)), ((), ())),
                              preferred_element_type=F32)
        p_ref[:, pl.ds(0, width)] = (acc + b_ref[:, pl.ds(0, width)]).astype(BF16)

    @pl.when(j < nsplit)
    def _():
        hb = _rms(x_ref[...], g_ref[...]).astype(BF16)
        r0 = pl.multiple_of(j * tx, tx)
        hb_ref[pl.ds(r0, tx), :] = hb
        gt = lax.dot_general(wif_ref[...], hb, (((1,), (1,)), ((), ())), preferred_element_type=F32)
        gate_ref[:, pl.ds(r0, tx)] = gt + bif_ref[...]

    if tail == tn:
        pl.when(j >= nsplit)(lambda: project(tn))
    else:
        pl.when(jnp.logical_and(j >= nsplit, j < last))(lambda: project(tn))
        pl.when(j == last)(lambda: project(tail))


def _inproj(x2, g, w_main, b_main, wif_t, bif_t, tm, tn, nsplit):
    t = x2.shape[0]
    tx = tm // nsplit
    ncol = pl.cdiv(N_MAIN, tn)
    tail = N_MAIN - (ncol - 1) * tn

    def col(j):
        return jnp.maximum(j - nsplit, 0)

    return pl.pallas_call(
        functools.partial(_inproj_kernel, nsplit=nsplit, tail=tail),
        grid=(t // tm, nsplit + ncol),
        in_specs=[pl.BlockSpec((tx, D_MODEL), lambda i, j: (i * nsplit + jnp.minimum(j, nsplit - 1), 0)),
                  pl.BlockSpec((1, D_MODEL), lambda i, j: (0, 0)),
                  pl.BlockSpec((tn, D_MODEL), lambda i, j: (col(j), 0)),
                  pl.BlockSpec((1, tn), lambda i, j: (0, col(j))),
                  pl.BlockSpec((GATE_ROWS, D_MODEL), lambda i, j: (0, 0)),
                  pl.BlockSpec((GATE_ROWS, 1), lambda i, j: (0, 0))],
        out_specs=[pl.BlockSpec((tm, tn), lambda i, j: (i, col(j))),
                   pl.BlockSpec((GATE_ROWS, tm), lambda i, j: (0, i))],
        out_shape=[jax.ShapeDtypeStruct((t, N_MAIN), BF16),
                   jax.ShapeDtypeStruct((GATE_ROWS, t), F32)],
        scratch_shapes=[pltpu.VMEM((tm, D_MODEL), BF16)],
        compiler_params=_params(("arbitrary", "arbitrary")),
        name="inproj",
    )(x2, g, w_main, b_main, wif_t, bif_t)


def _rope_kernel(pos_ref, freq_ref, cos_ref, sin_ref):
    half = RET_DQK // 2
    low = lax.broadcasted_iota(jnp.int32, (1, RET_DQK), 1) < half
    pos = jnp.where(low, pos_ref[:, 0:1], pos_ref[:, 1:2]).astype(F32)
    ang = pos * freq_ref[...]
    c = jnp.cos(ang)
    s = jnp.sin(ang)
    c_other = pltpu.roll(c, half, 1)
    s_other = pltpu.roll(s, half, 1)
    cos_ref[0] = jnp.where(low, c, c_other)
    cos_ref[1] = jnp.where(low, c_other, c)
    sin_ref[0] = jnp.where(low, -s, s_other)
    sin_ref[1] = jnp.where(low, -s_other, s)


def _rope_tables(pos_pair, freq2):
    th = pos_pair.shape[0]
    tm = min(ROPE_ROWS, th)
    out_spec = pl.BlockSpec((2, tm, RET_DQK), lambda i: (0, i, 0))
    out_shape = jax.ShapeDtypeStruct((2, th, RET_DQK), F32)
    cos2, sin2 = pl.pallas_call(
        _rope_kernel,
        grid=(th // tm,),
        in_specs=[pl.BlockSpec((tm, 2), lambda i: (i, 0)),
                  pl.BlockSpec((1, RET_DQK), lambda i: (0, 0))],
        out_specs=[out_spec, out_spec],
        out_shape=[out_shape, out_shape],
        compiler_params=_params(("arbitrary",)),
        name="rope",
    )(pos_pair, freq2)
    return cos2.reshape(2 * th, RET_DQK), sin2.reshape(2 * th, RET_DQK)


def _split3(x):
    hi = x.astype(BF16)
    r1 = x - hi.astype(F32)
    mid = r1.astype(BF16)
    lo = (r1 - mid.astype(F32)).astype(BF16)
    return hi, mid, lo


def _conv_shift_matrix():
    s = np.zeros(((ML_CONV - 1) * CONV_BLOCK, 2 * CONV_BLOCK), np.float32)
    t = np.arange(CONV_BLOCK)
    for j in range(ML_CONV - 1):
        s[j * CONV_BLOCK + t, CONV_BLOCK + t - (ML_CONV - 1) + j] = 1.0
    return s


def _mlstm_setup(q_ref, k_ref, v_ref, o_ref, z_ref, gi_ref, gf_ref, cwq_ref, cwk_ref, cbq_ref, cbk_ref,
                 hg_ref, shift_ref, out_ref, pad_ref, qc_ref, kc_ref, u_ref, mr_ref, a_ref, w_ref, mp_ref, mn_ref,
                 dec_ref, caug_ref, *, seq):
    L = CHUNK
    nchunk = seq // L

    pad_ref[pl.ds(0, CONV_BLOCK), :] = jnp.zeros((CONV_BLOCK, 2 * ML_DQK), BF16)
    pad_ref[pl.ds(CONV_BLOCK, seq), pl.ds(0, ML_DQK)] = q_ref[...]
    pad_ref[pl.ds(CONV_BLOCK, seq), pl.ds(ML_DQK, ML_DQK)] = k_ref[...]
    cw = jnp.concatenate([cwq_ref[...], cwk_ref[...]], axis=-1)
    cb = jnp.concatenate([cbq_ref[...], cbk_ref[...]], axis=-1)
    shift = shift_ref[...]

    def conv_block(i, carry):
        r = pl.multiple_of(i * CONV_BLOCK, CONV_BLOCK)
        taps = _dot(shift, pad_ref[pl.ds(r, 2 * CONV_BLOCK), :])
        y = cb + pad_ref[pl.ds(r + CONV_BLOCK, CONV_BLOCK), :].astype(F32) * cw[ML_CONV - 1:ML_CONV, :]
        for j in range(ML_CONV - 1):
            y = y + taps[j * CONV_BLOCK:(j + 1) * CONV_BLOCK, :] * cw[j:j + 1, :]
        yh = 0.5 * y
        y = yh * (1.0 + jnp.tanh(yh))
        qc_ref[pl.ds(r, CONV_BLOCK), :] = y[:, :ML_DQK] * (ML_DQK ** -0.5)
        kc_ref[pl.ds(r, CONV_BLOCK), :] = y[:, ML_DQK:]
        return carry

    lax.fori_loop(0, seq // CONV_BLOCK, conv_block, 0, unroll=16)

    f = gf_ref[0]
    lf = jnp.minimum(f, 0.0) - jnp.log1p(jnp.exp(-jnp.abs(f)))
    row = lax.broadcasted_iota(jnp.int32, (L, L), 0)
    col = lax.broadcasted_iota(jnp.int32, (L, L), 1)
    triu = (row <= col).astype(BF16)
    hi, mid, lo = _split3(lf)
    bb = _dot(hi, triu) + _dot(mid, triu) + _dot(lo, triu)
    causal = row >= col
    diag = row == col

    gi = gi_ref[0]
    g_col = bb[:, L - 1:L]
    log_w = g_col - bb + gi
    w_max = jnp.max(log_w, axis=-1, keepdims=True)
    m = jnp.zeros((1, 1), F32)
    for c in range(nchunk):
        mp_ref[c:c + 1, :] = m
        m = jnp.maximum(g_col[c:c + 1, :] + m, w_max[c:c + 1, :])
        mn_ref[c:c + 1, :] = m
    m_prev_all = mp_ref[...]
    m_new_all = mn_ref[...]
    a_all = gi - bb
    a_ref[...] = a_all
    w_ref[...] = jnp.exp(log_w - m_new_all)
    dec_ref[...] = jnp.exp(g_col + m_prev_all - m_new_all)
    lane = lax.broadcasted_iota(jnp.int32, a_all.shape, 1)
    run = a_all
    shift_by = 1
    while shift_by < L:
        run = jnp.maximum(run, jnp.where(lane >= shift_by, pltpu.roll(run, shift_by, 1), -jnp.inf))
        shift_by *= 2
    u_all = jnp.maximum(m_prev_all, run)
    u_ref[...] = u_all
    mr_ref[...] = bb + u_all
    eye3 = jnp.tile(diag.astype(BF16), (1, 3))

    caug_ref[...] = jnp.zeros_like(caug_ref)
    hg_half = 0.5 * hg_ref[...]
    ones_blk = jnp.ones((L, LANES), BF16)

    def chunk(c):
        r0 = pl.multiple_of(c * L, L)
        rows = pl.ds(r0, L)
        one = pl.ds(c, 1)
        a_row = a_ref[one, :]
        w_row = w_ref[one, :]
        m_prev = mp_ref[one, :]
        decay = dec_ref[one, :]
        u_pieces = jnp.concatenate(_split3(u_ref[one, :]), axis=1)
        m_pieces = jnp.concatenate(_split3(mr_ref[one, :]), axis=1)
        rows_t = jnp.concatenate([jnp.broadcast_to(u_pieces, (LANES, 3 * L)),
                                  jnp.broadcast_to(m_pieces, (LANES, 3 * L))], axis=0)
        um = lax.dot_general(eye3, rows_t, (((1,), (1,)), ((), ())), preferred_element_type=F32)
        u_rep = um[:, :LANES]
        d = jnp.exp(jnp.where(causal, a_row, -jnp.inf) - jnp.tile(u_rep, (1, L // LANES)))
        inter = jnp.exp(m_prev - u_rep)
        qf = qc_ref[rows, :]
        kf = kc_ref[rows, :]
        qb = qf.astype(BF16)
        kt = kf.T
        vb = v_ref[rows, :]
        s = _dot(qb, kt.astype(BF16)) * d
        caug = caug_ref[...]
        vaug = jnp.concatenate([vb, ones_blk], axis=1)
        na = _dot(s.astype(BF16), vaug) + jnp.tile(inter, (1, ML_DV // LANES + 1)) * _dot(qb, caug.astype(BF16))
        den = jnp.maximum(jnp.abs(na[:, ML_DV:]), jnp.exp(-um[:, LANES:]))

        caug_ref[...] = decay * caug + _dot((kt * w_row).astype(BF16), vaug)

        num = na[:, :ML_DV]
        cen = num - jnp.mean(num, axis=-1, keepdims=True)
        var = jnp.mean(cen * cen, axis=-1, keepdims=True)
        y = cen * jnp.tile(lax.rsqrt(var + EPS * (den * den)), (1, ML_DV // LANES)) * hg_half
        oh = 0.5 * o_ref[rows, :].astype(F32)
        zh = 0.5 * z_ref[rows, :].astype(F32)
        y = y * (1.0 + jnp.tanh(oh)) * (zh * (1.0 + jnp.tanh(zh)))
        out_ref[rows, :] = y.astype(BF16)

    return chunk


def _ret_setup(lg_ref, q_ref, k_ref, v_ref, z_ref, cos_ref, sin_ref, hg_ref, out_ref, r_ref):
    L = CHUNK
    lg = lg_ref[pl.program_id(1)]
    row = lax.broadcasted_iota(jnp.int32, (L, L), 0)
    col = lax.broadcasted_iota(jnp.int32, (L, L), 1)
    intra = jnp.where(row >= col, jnp.exp((row - col).astype(F32) * lg), 0.0)
    pos_col = lax.broadcasted_iota(jnp.int32, (L, 1), 0).astype(F32)
    pos_row = lax.broadcasted_iota(jnp.int32, (1, L), 1).astype(F32)
    q_decay = jnp.exp((pos_col + 1.0) * lg)
    k_decay = jnp.exp((L - 1.0 - pos_row) * lg)
    chunk_decay = jnp.exp(jnp.full((1, 1), float(L), F32) * lg)
    r_ref[...] = jnp.zeros_like(r_ref)
    half = RET_DQK // 2
    hg = hg_ref[...]

    def chunk(c):
        r0 = pl.multiple_of(c * L, L)
        rows = pl.ds(r0, L)
        cs = cos_ref[rows, :]
        sn = sin_ref[rows, :]
        qf = q_ref[rows, :].astype(F32)
        kf = k_ref[rows, :].astype(F32)
        qr = qf * cs + pltpu.roll(qf, half, 1) * sn
        kr = (kf * cs + pltpu.roll(kf, half, 1) * sn) * (RET_DQK ** -0.5)
        qb = qr.astype(BF16)
        kt = kr.T
        vb = v_ref[rows, :]
        s = _dot(qb, kt.astype(BF16)) * intra
        r_state = r_ref[...]
        o = _dot(s.astype(BF16), vb) + q_decay * _dot(qb, r_state.astype(BF16))
        r_ref[...] = chunk_decay * r_state + _dot((kt * k_decay).astype(BF16), vb)
        y = _head_norm(o, hg)
        zh = 0.5 * z_ref[rows, :].astype(F32)
        out_ref[rows, :] = (y * (zh * (1.0 + jnp.tanh(zh)))).astype(BF16)

    return chunk


N_ML_IN = 13
N_RT_IN = 7
N_ML_SCRATCH = 11
CHUNK_UNROLL = 4


def _mixers_kernel(lg_ref, *refs, seq):
    ml_in = refs[:N_ML_IN]
    rt_in = refs[N_ML_IN:N_ML_IN + N_RT_IN]
    ml_out, rt_out = refs[N_ML_IN + N_RT_IN:N_ML_IN + N_RT_IN + 2]
    scratch = refs[N_ML_IN + N_RT_IN + 2:]
    nchunk = seq // CHUNK
    ml_chunk = _mlstm_setup(*ml_in, ml_out, *scratch[:N_ML_SCRATCH], seq=seq)
    lax.fori_loop(0, nchunk, lambda c, carry: (ml_chunk(c), carry)[1], 0, unroll=CHUNK_UNROLL)
    rt_chunk = _ret_setup(lg_ref, *rt_in, rt_out, scratch[N_ML_SCRATCH])
    lax.fori_loop(0, nchunk, lambda c, carry: (rt_chunk(c), carry)[1], 0, unroll=CHUNK_UNROLL)


def _mixers(log_gamma, p, gates3, conv_w, conv_b, ml_g, cos2, sin2, rt_g, batch, seq):
    t = batch * seq
    nchunk = seq // CHUNK
    qb, vb = ML_DQK, ML_DV
    assert (ML_HEADS, ML_DQK, ML_DV) == (RET_HEADS, RET_DQK, RET_DV)

    def pspec(width, off):
        return pl.BlockSpec((seq, width), lambda b, h, lg, off=off // width: (b, off + h))

    def gspec(row0):
        return pl.BlockSpec((1, nchunk, CHUNK), lambda b, h, lg, row0=row0: (row0 + h, b, 0))

    ml_specs = [pspec(qb, OFF_ML_Q), pspec(qb, OFF_ML_K), pspec(vb, OFF_ML_V), pspec(vb, OFF_ML_O),
                pspec(vb, OFF_ML_Z), gspec(0), gspec(F_ROW),
                pl.BlockSpec((ML_CONV, qb), lambda b, h, lg: (0, h)),
                pl.BlockSpec((ML_CONV, qb), lambda b, h, lg: (0, ML_HEADS + h)),
                pl.BlockSpec((1, qb), lambda b, h, lg: (0, h)),
                pl.BlockSpec((1, qb), lambda b, h, lg: (0, ML_HEADS + h)),
                pl.BlockSpec((1, vb), lambda b, h, lg: (0, h)),
                pl.BlockSpec(((ML_CONV - 1) * CONV_BLOCK, 2 * CONV_BLOCK), lambda b, h, lg: (0, 0))]
    rt_specs = [pspec(qb, OFF_RT_Q), pspec(qb, OFF_RT_K), pspec(vb, OFF_RT_V), pspec(vb, OFF_RT_Z),
                pl.BlockSpec((seq, qb), lambda b, h, lg: (b, 0)),
                pl.BlockSpec((seq, qb), lambda b, h, lg: (b, 0)),
                pl.BlockSpec((1, vb), lambda b, h, lg: (0, h))]
    assert len(ml_specs) == N_ML_IN and len(rt_specs) == N_RT_IN
    ml_scratch = [pltpu.VMEM((seq + CONV_BLOCK, 2 * qb), BF16),
                  pltpu.VMEM((seq, qb), F32),
                  pltpu.VMEM((seq, qb), F32),
                  pltpu.VMEM((nchunk, CHUNK), F32),
                  pltpu.VMEM((nchunk, CHUNK), F32),
                  pltpu.VMEM((nchunk, CHUNK), F32),
                  pltpu.VMEM((nchunk, CHUNK), F32),
                  pltpu.VMEM((nchunk, 1), F32),
                  pltpu.VMEM((nchunk, 1), F32),
                  pltpu.VMEM((nchunk, 1), F32),
                  pltpu.VMEM((qb, vb + LANES), F32)]
    assert len(ml_scratch) == N_ML_SCRATCH
    out_spec = pl.BlockSpec((seq, vb), lambda b, h, lg: (b, h))
    grid_spec = pltpu.PrefetchScalarGridSpec(
        num_scalar_prefetch=1,
        grid=(batch, ML_HEADS),
        in_specs=ml_specs + rt_specs,
        out_specs=[out_spec, out_spec],
        scratch_shapes=ml_scratch + [pltpu.VMEM((qb, vb), F32)],
    )
    return pl.pallas_call(
        functools.partial(_mixers_kernel, seq=seq),
        grid_spec=grid_spec,
        out_shape=[jax.ShapeDtypeStruct((t, ML_V), BF16), jax.ShapeDtypeStruct((t, RET_V), BF16)],
        compiler_params=_params(("arbitrary", "arbitrary")),
        name="mixers",
    )(log_gamma, p, p, p, p, p, gates3, gates3, conv_w, conv_w, conv_b, conv_b, ml_g,
      jnp.asarray(_conv_shift_matrix(), BF16), p, p, p, p, cos2, sin2, rt_g)


def _xattn_kernel(q_ref, z_ref, mk_ref, mv_ref, out_ref, *, blk):
    mk = mk_ref[...]
    mv = mv_ref[...]

    def block(i, carry):
        rows = pl.ds(pl.multiple_of(i * blk, blk), blk)
        sc = lax.dot_general(q_ref[rows, :], mk, (((1,), (1,)), ((), ())), preferred_element_type=F32)
        sc = sc * (XA_DH ** -0.5)
        e = jnp.exp(sc - jnp.max(sc, axis=-1, keepdims=True))
        o = _dot(e.astype(BF16), mv) * (1.0 / jnp.sum(e, axis=-1, keepdims=True))
        zh = 0.5 * z_ref[rows, :].astype(F32)
        out_ref[rows, :] = (o * (zh * (1.0 + jnp.tanh(zh)))).astype(BF16)
        return carry

    lax.fori_loop(0, q_ref.shape[0] // blk, block, 0, unroll=2)


def _xattn(p, memkv, batch, seq):
    t = batch * seq
    ts = seq
    ns = seq // ts
    w = XA_DH
    return pl.pallas_call(
        functools.partial(_xattn_kernel, blk=min(XATTN_BLOCK, ts)),
        grid=(batch, XA_HEADS, ns),
        in_specs=[pl.BlockSpec((ts, w), lambda b, h, s: (b * ns + s, OFF_XA_Q // w + h)),
                  pl.BlockSpec((ts, w), lambda b, h, s: (b * ns + s, OFF_XA_Z // w + h)),
                  pl.BlockSpec((MEM_TOKENS, w), lambda b, h, s: (b, h)),
                  pl.BlockSpec((MEM_TOKENS, w), lambda b, h, s: (b, XA_HEADS + h))],
        out_specs=pl.BlockSpec((ts, w), lambda b, h, s: (b * ns + s, h)),
        out_shape=jax.ShapeDtypeStruct((t, XA_W), BF16),
        compiler_params=_params(("arbitrary", "arbitrary", "arbitrary")),
        name="xattn",
    )(p, p, memkv, memkv)


def _merge_kernel(ml_ref, rt_ref, xa_ref, wml_ref, wrt_ref, wxa_ref, *refs):
    gate_refs, o_ref = refs[:-1], refs[-1]
    pieces = len(gate_refs) // N_BRANCH
    acc = None
    for br, (a_ref, w_ref) in enumerate(((ml_ref, wml_ref), (rt_ref, wrt_ref), (xa_ref, wxa_ref))):
        gate = jnp.concatenate([g[...] for g in gate_refs[br * pieces:(br + 1) * pieces]], axis=1)
        term = _sigmoid(gate.astype(F32)) * _dot(a_ref[...], w_ref[...])
        acc = term if acc is None else acc + term
    o_ref[...] = acc.astype(BF16)


def _merge(ml_out, rt_out, xa_out, w_ml, w_rt, w_xa, p, tm, tn):
    t = ml_out.shape[0]
    nj = D_MODEL // tn
    tg = MERGE_GATE_COLS
    assert OFF_GATE % tg == 0 and D_MODEL % tg == 0 and tn % tg == 0
    pieces = tn // tg

    def gspec(br, piece):
        first = (OFF_GATE + br * D_MODEL) // tg + piece
        return pl.BlockSpec((tm, tg), lambda i, j: (i, first + j * pieces))

    return pl.pallas_call(
        _merge_kernel,
        grid=(t // tm, nj),
        in_specs=[pl.BlockSpec((tm, ML_V), lambda i, j: (i, 0)),
                  pl.BlockSpec((tm, RET_V), lambda i, j: (i, 0)),
                  pl.BlockSpec((tm, XA_W), lambda i, j: (i, 0)),
                  pl.BlockSpec((ML_V, tn), lambda i, j: (0, j)),
                  pl.BlockSpec((RET_V, tn), lambda i, j: (0, j)),
                  pl.BlockSpec((XA_W, tn), lambda i, j: (0, j))]
        + [gspec(br, piece) for br in range(N_BRANCH) for piece in range(pieces)],
        out_specs=pl.BlockSpec((tm, tn), lambda i, j: (i, j)),
        out_shape=jax.ShapeDtypeStruct((t, D_MODEL), BF16),
        compiler_params=_params(("arbitrary", "arbitrary")),
        name="merge",
    )(ml_out, rt_out, xa_out, w_ml, w_rt, w_xa, *([p] * (N_BRANCH * pieces)))


def _outproj_kernel(x_ref, m_ref, w_ref, g_ref, o_ref):
    y = x_ref[...] + _dot(m_ref[...], w_ref[...])
    o_ref[...] = _rms(y, g_ref[...])


def _outproj(x2, merged, w_out, final_g, tm):
    t = x2.shape[0]
    return pl.pallas_call(
        _outproj_kernel,
        grid=(t // tm,),
        in_specs=[pl.BlockSpec((tm, D_MODEL), lambda i: (i, 0)),
                  pl.BlockSpec((tm, D_MODEL), lambda i: (i, 0)),
                  pl.BlockSpec((D_MODEL, D_MODEL), lambda i: (0, 0), pipeline_mode=pl.Buffered(1)),
                  pl.BlockSpec((1, D_MODEL), lambda i: (0, 0))],
        out_specs=pl.BlockSpec((tm, D_MODEL), lambda i: (i, 0)),
        out_shape=jax.ShapeDtypeStruct((t, D_MODEL), F32),
        compiler_params=_params(("arbitrary",), OUTPROJ_VMEM_LIMIT),
        name="outproj",
    )(x2, merged, w_out, final_g)


def kernel(x, mem, positions, ln_g, mem_ln_g, w_in, b_in, conv_w, conv_b, ml_hnorm_g, ret_hnorm_g, w_mem_kv,
           w_br_ml, w_br_ret, w_br_xa, w_out, final_g):
    batch, seq, _ = x.shape
    assert seq % CHUNK == 0 and ln_g.shape[0] == 1
    t = batch * seq
    x2 = x.reshape(t, D_MODEL)

    wt, b0 = w_in[0].T, b_in[0]
    w_full = wt.astype(BF16)
    w_main = lax.dynamic_update_slice(w_full, w_full[OFF_IF + 2 * ML_HEADS:], (OFF_IF, 0))
    b_main = jnp.concatenate([b0[:OFF_IF], b0[OFF_IF + 2 * ML_HEADS:]])[None, :]
    wif_t = jnp.zeros((GATE_ROWS, D_MODEL), F32)
    wif_t = wif_t.at[0:ML_HEADS].set(wt[OFF_IF:OFF_IF + ML_HEADS])
    wif_t = wif_t.at[F_ROW:F_ROW + ML_HEADS].set(wt[OFF_IF + ML_HEADS:OFF_IF + 2 * ML_HEADS]).astype(BF16)
    bif_t = jnp.zeros((GATE_ROWS, 1), F32)
    bif_t = bif_t.at[0:ML_HEADS, 0].set(b0[OFF_IF:OFF_IF + ML_HEADS])
    bif_t = bif_t.at[F_ROW:F_ROW + ML_HEADS, 0].set(b0[OFF_IF + ML_HEADS:OFF_IF + 2 * ML_HEADS])

    memkv = _memkv(mem.reshape(batch * MEM_TOKENS, D_MODEL), mem_ln_g[0][None, :], w_mem_kv[0].astype(BF16))

    p, gates_t = _inproj(x2, ln_g[0][None, :], w_main, b_main, wif_t, bif_t, min(INPROJ_ROWS, t), INPROJ_COLS,
                         INPROJ_NORM_STEPS)
    gates3 = gates_t.reshape(GATE_ROWS, t // CHUNK, CHUNK)

    half = RET_DQK // 2
    freqs = ROPE_BASE ** (-jnp.arange(half, dtype=F32) / half)
    freq2 = jnp.concatenate([freqs, freqs])[None, :]
    cos2, sin2 = _rope_tables(positions.reshape(2, t // 2).T, freq2)
    log_gamma = jnp.asarray(np.log(1.0 - 2.0 ** (-5.0 - np.arange(RET_HEADS))), dtype=F32)

    ml_out, rt_out = _mixers(log_gamma, p, gates3, conv_w[0], conv_b[0][None, :], ml_hnorm_g[0][None, :],
                             cos2, sin2, ret_hnorm_g[0][None, :], batch, seq)
    xa_out = _xattn(p, memkv, batch, seq)

    merged = _merge(ml_out, rt_out, xa_out, w_br_ml[0].astype(BF16), w_br_ret[0].astype(BF16),
                    w_br_xa[0].astype(BF16), p, min(MERGE_ROWS, t), MERGE_COLS)
    out = _outproj(x2, merged, w_out[0].astype(BF16), final_g[None, :], min(OUTPROJ_ROWS, t))
    return out.reshape(batch, seq, D_MODEL)
```
